```python
import math
import jax, jax.numpy as jnp
from jax import lax
import numpy as np

D_MODEL = 1024
BATCH = 8
SEQ = 8192
DEPTH = 2

GRID_W = 64
MEM_TOKENS = 256
HEAD_DIM = 64
D_MIX = D_MODEL
NA_DIM = D_MIX // 4
NA_HEADS = NA_DIM // HEAD_DIM
NA_WIN_ROWS = 8
NA_WIN_COLS = 16
SSD_DIM = D_MIX // 2
SSD_HEADS = SSD_DIM // HEAD_DIM
SSD_GROUPS = 2
SSD_STATE = 64
SSD_CONV = 5
SSD_CHUNK = 128
SSD_CONV_DIM = SSD_DIM + 2 * SSD_GROUPS * SSD_STATE
S5_DIM = D_MIX - NA_DIM - SSD_DIM
S5_GROUP_CH = 16
S5_GROUPS = S5_DIM // S5_GROUP_CH
S5_STATE = 64
XA_HEADS = 4
XA_HEAD_DIM = D_MODEL // XA_HEADS
D_FF = 4 * D_MODEL
IN_SIZES = (NA_DIM, NA_DIM, NA_DIM, SSD_DIM, SSD_CONV_DIM, 2 * SSD_HEADS, S5_DIM)
D_IN_PROJ = sum(IN_SIZES)
LN_EPS = 1e-5
NEG_BIG = -1e30
DEEPNORM_ALPHA = (2 * DEPTH) ** 0.25
DEEPNORM_BETA = (8 * DEPTH) ** -0.25

kernel_name = 'hybrid_na_ssd_s5_encoder'


def _split_cols(t, sizes):
    offs, acc = [], 0
    for s in sizes[:-1]:
        acc += s
        offs.append(acc)
    return jnp.split(t, offs, axis=-1)


def layernorm(x, g, b):
    xf = x.astype(jnp.float32)
    mu = jnp.mean(xf, axis=-1, keepdims=True)
    var = jnp.mean(jnp.square(xf - mu), axis=-1, keepdims=True)
    y = (xf - mu) * lax.rsqrt(var + LN_EPS) * g.astype(jnp.float32) + b.astype(jnp.float32)
    return y.astype(x.dtype)


def neighbourhood_attention(q, k, v, rpb):
    bsz, t, nh, dh = q.shape
    rows = t // GRID_W
    wr = min(NA_WIN_ROWS, rows)
    q = q.reshape(bsz, rows, GRID_W, nh, dh)
    k = k.reshape(bsz, rows, GRID_W, nh, dh)
    v = v.reshape(bsz, rows, GRID_W, nh, dh)
    r = jnp.arange(rows)
    r0 = jnp.clip(r - wr // 2, 0, rows - wr)
    row_idx = r0[:, None] + jnp.arange(wr)[None, :]
    kg = jnp.take(k, row_idx, axis=1)
    vg = jnp.take(v, row_idx, axis=1)
    c = jnp.arange(GRID_W)
    c0 = jnp.clip(c - NA_WIN_COLS // 2, 0, GRID_W - NA_WIN_COLS)
    col_mask = (c[None, :] >= c0[:, None]) & (c[None, :] < c0[:, None] + NA_WIN_COLS)
    ri = row_idx - r[:, None] + (NA_WIN_ROWS - 1)
    ci = jnp.clip(c[None, :] - c[:, None], -(NA_WIN_COLS - 1), NA_WIN_COLS - 1) + (NA_WIN_COLS - 1)
    bias = rpb.astype(jnp.float32)[:, ri[:, None, :, None], ci[None, :, None, :]]
    s = jnp.einsum('brqhd,brjkhd->bhrqjk', q, kg).astype(jnp.float32) * (dh ** -0.5) + bias
    s = jnp.where(col_mask[:, None, :], s, NEG_BIG)
    p = jax.nn.softmax(s.reshape(bsz, nh, rows, GRID_W, wr * GRID_W), axis=-1)
    p = p.reshape(bsz, nh, rows, GRID_W, wr, GRID_W).astype(v.dtype)
    o = jnp.einsum('bhrqjk,brjkhd->brqhd', p, vg)
    return o.reshape(bsz, t, nh * dh)


def ssd_scan(x, dt, a, bm, cm):
    bsz, t, nh, hp = x.shape
    nc = t // SSD_CHUNK
    qc = SSD_CHUNK
    rep = nh // SSD_GROUPS
    xdt = (x * dt[..., None]).reshape(bsz, nc, qc, nh, hp)
    bg = bm.reshape(bsz, nc, qc, SSD_GROUPS, SSD_STATE)
    cg = cm.reshape(bsz, nc, qc, SSD_GROUPS, SSD_STATE)
    acs = jnp.cumsum((dt * a).reshape(bsz, nc, qc, nh), axis=2)
    acs_h = acs.transpose(0, 3, 1, 2)
    seg = acs_h[..., :, None] - acs_h[..., None, :]
    tril = jnp.tril(jnp.ones((qc, qc), dtype=bool))
    decay_ls = jnp.exp(jnp.where(tril, seg, -jnp.inf))
    cb = jnp.repeat(jnp.einsum('bclgn,bcsgn->bgcls', cg, bg), rep, axis=1)
    y_diag = jnp.einsum('bhcls,bcshp->bclhp', cb * decay_ls, xdt)
    bh = jnp.repeat(bg, rep, axis=3)
    ch = jnp.repeat(cg, rep, axis=3)
    decay_to_end = jnp.exp(acs[:, :, -1:, :] - acs)
    states = jnp.einsum('bcshn,bcshp->bchpn', bh * decay_to_end[..., None], xdt)
    chunk_decay = jnp.exp(acs[:, :, -1, :])

    def step(h, inp):
        s_c, d_c = inp
        return h * d_c[:, :, None, None] + s_c, h

    h0 = jnp.zeros((bsz, nh, hp, SSD_STATE), jnp.float32)
    _, prev = lax.scan(step, h0, (states.transpose(1, 0, 2, 3, 4), chunk_decay.transpose(1, 0, 2)))
    prev = prev.transpose(1, 0, 2, 3, 4)
    y_off = jnp.einsum('bclhn,bchpn->bclhp', ch * jnp.exp(acs)[..., None], prev)
    return (y_diag + y_off).reshape(bsz, t, nh, hp)


def ssd_mixer(z, xbc, dt_raw, conv_w, conv_b, dt_bias, a_log, d_skip, norm_w):
    bsz, t, _ = z.shape
    xbc = lax.conv_general_dilated(
        xbc, conv_w[:, None, :].astype(xbc.dtype), window_strides=(1,),
        padding=[(SSD_CONV // 2, SSD_CONV // 2)], dimension_numbers=('NWC', 'WIO', 'NWC'),
        feature_group_count=SSD_CONV_DIM)
    xbc = jax.nn.silu((xbc + conv_b).astype(jnp.float32))
    xs, bm, cm = jnp.split(xbc, [SSD_DIM, SSD_DIM + SSD_GROUPS * SSD_STATE], axis=-1)
    xs = xs.reshape(bsz, t, SSD_HEADS, HEAD_DIM)
    bm = bm.reshape(bsz, t, SSD_GROUPS, SSD_STATE)
    cm = cm.reshape(bsz, t, SSD_GROUPS, SSD_STATE)
    dt = jax.nn.softplus(dt_raw.astype(jnp.float32).reshape(bsz, t, 2, SSD_HEADS) + dt_bias.astype(jnp.float32))
    a = -jnp.exp(a_log.astype(jnp.float32))
    flip = lambda u: jnp.flip(u, axis=1)
    y_fwd = ssd_scan(xs, dt[:, :, 0], a[0], bm, cm)
    y_bwd = flip(ssd_scan(flip(xs), flip(dt[:, :, 1]), a[1], flip(bm), flip(cm)))
    y = y_fwd + y_bwd + d_skip.astype(jnp.float32)[:, None] * xs
    y = y.reshape(bsz, t, SSD_DIM) * jax.nn.silu(z.astype(jnp.float32))
    y = y * lax.rsqrt(jnp.mean(jnp.square(y), axis=-1, keepdims=True) + LN_EPS) * norm_w.astype(jnp.float32)
    return y.astype(z.dtype)


def _cmul_scan_op(e1, e2):
    a1r, a1i, b1r, b1i = e1
    a2r, a2i, b2r, b2i = e2
    return (a2r * a1r - a2i * a1i,
            a2r * a1i + a2i * a1r,
            a2r * b1r - a2i * b1i + b2r,
            a2r * b1i + a2i * b1r + b2i)


def s5_mixer(u, lam_re, lam_im, log_dt, b_re, b_im, c_re, c_im, d_skip, glu_w, glu_b):
    bsz, t, _ = u.shape
    uf = u.astype(jnp.float32)
    ug = uf.reshape(bsz, t, S5_GROUPS, S5_GROUP_CH).transpose(1, 0, 2, 3)
    y = d_skip.astype(jnp.float32) * uf
    for direction in range(2):
        dt = jnp.exp(log_dt[direction].astype(jnp.float32))[:, None]
        lr = lam_re[direction].astype(jnp.float32)
        li = lam_im[direction].astype(jnp.float32)
        mag = jnp.exp(lr * dt)
        ar, ai = mag * jnp.cos(li * dt), mag * jnp.sin(li * dt)
        den = lr * lr + li * li
        fr = ((ar - 1.0) * lr + ai * li) / den
        fi = (ai * lr - (ar - 1.0) * li) / den
        br, bi = b_re[direction].astype(jnp.float32), b_im[direction].astype(jnp.float32)
        bbr = fr[..., None] * br - fi[..., None] * bi
        bbi = fr[..., None] * bi + fi[..., None] * br
        bu_re = jnp.einsum('gph,tbgh->tbgp', bbr, ug)
        bu_im = jnp.einsum('gph,tbgh->tbgp', bbi, ug)
        a_re = jnp.broadcast_to(ar, (t, 1, S5_GROUPS, S5_STATE))
        a_im = jnp.broadcast_to(ai, (t, 1, S5_GROUPS, S5_STATE))
        _, _, xr, xi = lax.associative_scan(_cmul_scan_op, (a_re, a_im, bu_re, bu_im),
                                            axis=0, reverse=(direction == 1))
        yd = (jnp.einsum('ghp,tbgp->btgh', c_re[direction].astype(jnp.float32), xr)
              - jnp.einsum('ghp,tbgp->btgh', c_im[direction].astype(jnp.float32), xi))
        y = y + yd.reshape(bsz, t, S5_DIM)
    g = jax.nn.gelu(y)
    out = g * jax.nn.sigmoid(g @ glu_w.astype(jnp.float32) + glu_b.astype(jnp.float32))
    return out.astype(u.dtype)


def hybrid_mixer(h, w_in, rpb, conv_w, conv_b, dt_bias, a_log, d_ssd, ssd_norm_w,
                 lam_re, lam_im, log_dt, b_re, b_im, c_re, c_im, d_s5, glu_w, glu_b, w_out):
    bsz, t, _ = h.shape
    proj = h @ w_in
    q, k, v, z, xbc, dt_raw, u = _split_cols(proj, IN_SIZES)
    heads = lambda a: a.reshape(bsz, t, NA_HEADS, HEAD_DIM)
    o_na = neighbourhood_attention(heads(q), heads(k), heads(v), rpb).astype(h.dtype)
    o_ssd = ssd_mixer(z, xbc, dt_raw, conv_w, conv_b, dt_bias, a_log, d_ssd, ssd_norm_w)
    o_s5 = s5_mixer(u, lam_re, lam_im, log_dt, b_re, b_im, c_re, c_im, d_s5, glu_w, glu_b)
    mixed = jnp.concatenate([o_na, o_ssd, o_s5], axis=-1)
    return mixed @ w_out


def cross_attention(h, mem, wq, wk, wv, wo):
    bsz, t, _ = h.shape
    m = mem.shape[1]
    q = (h @ wq).reshape(bsz, t, XA_HEADS, XA_HEAD_DIM)
    k = (mem @ wk).reshape(bsz, m, XA_HEADS, XA_HEAD_DIM)
    v = (mem @ wv).reshape(bsz, m, XA_HEADS, XA_HEAD_DIM)
    s = jnp.einsum('bthd,bmhd->bhtm', q, k).astype(jnp.float32) * (XA_HEAD_DIM ** -0.5)
    p = jax.nn.softmax(s, axis=-1).astype(v.dtype)
    o = jnp.einsum('bhtm,bmhd->bthd', p, v).reshape(bsz, t, XA_HEADS * XA_HEAD_DIM)
    return o @ wo


def squared_relu_mlp(h, w1, w2):
    return jnp.square(jax.nn.relu(h @ w1)) @ w2


def setup_inputs(seed: int = 0) -> dict:
    key = jax.random.key(seed)
    ks = jax.random.split(key, 48)
    f32 = jnp.float32
    L = DEPTH

    def normal(k, shape, scale):
        return scale * jax.random.normal(k, shape, f32)

    def uniform(k, shape, lo, hi):
        return jax.random.uniform(k, shape, f32, minval=lo, maxval=hi)

    x = normal(ks[0], (BATCH, SEQ, D_MODEL), 1.0)
    mem = normal(ks[1], (BATCH, MEM_TOKENS, D_MODEL), 1.0)
    ln_in_g = 1.0 + normal(ks[2], (D_MODEL,), 0.02)
    ln_in_b = normal(ks[3], (D_MODEL,), 0.02)
    w_in = normal(ks[4], (L, D_MODEL, D_IN_PROJ), D_MODEL ** -0.5)
    na_rpb = normal(ks[5], (L, NA_HEADS, 2 * NA_WIN_ROWS - 1, 2 * NA_WIN_COLS - 1), 0.02)
    ssd_conv_w = normal(ks[6], (L, SSD_CONV, SSD_CONV_DIM), SSD_CONV ** -0.5)
    ssd_conv_b = normal(ks[7], (L, SSD_CONV_DIM), 0.02)
    dt0 = jnp.exp(uniform(ks[8], (L, 2, SSD_HEADS), math.log(1e-3), math.log(1e-1)))
    ssd_dt_bias = dt0 + jnp.log(-jnp.expm1(-dt0))
    ssd_a_log = jnp.log(uniform(ks[9], (L, 2, SSD_HEADS), 1.0, 16.0))
    ssd_d = 1.0 + normal(ks[10], (L, SSD_HEADS), 0.02)
    ssd_norm_w = 1.0 + normal(ks[11], (L, SSD_DIM), 0.02)
    s5_lam_re = -0.5 + normal(ks[12], (L, 2, S5_GROUPS, S5_STATE), 0.01)
    s5_lam_im = math.pi * jnp.arange(S5_STATE, dtype=f32) + normal(ks[13], (L, 2, S5_GROUPS, S5_STATE), 0.01)
    s5_log_dt = uniform(ks[14], (L, 2, S5_GROUPS), math.log(1e-3), math.log(1e-1))
    s5_b_re = normal(ks[15], (L, 2, S5_GROUPS, S5_STATE, S5_GROUP_CH), (2 * S5_GROUP_CH) ** -0.5)
    s5_b_im = normal(ks[16], (L, 2, S5_GROUPS, S5_STATE, S5_GROUP_CH), (2 * S5_GROUP_CH) ** -0.5)
    s5_c_re = normal(ks[17], (L, 2, S5_GROUPS, S5_GROUP_CH, S5_STATE), (2 * S5_STATE) ** -0.5)
    s5_c_im = normal(ks[18], (L, 2, S5_GROUPS, S5_GROUP_CH, S5_STATE), (2 * S5_STATE) ** -0.5)
    s5_d = normal(ks[19], (L, S5_DIM), 1.0)
    s5_glu_w = normal(ks[20], (L, S5_DIM, S5_DIM), S5_DIM ** -0.5)
    s5_glu_b = normal(ks[21], (L, S5_DIM), 0.02)
    w_mix_out = normal(ks[22], (L, D_MIX, D_MODEL), D_MIX ** -0.5 * DEEPNORM_BETA)
    ln_mix_g = 1.0 + normal(ks[23], (L, D_MODEL), 0.02)
    ln_mix_b = normal(ks[24], (L, D_MODEL), 0.02)
    xa_wq = normal(ks[25], (L, D_MODEL, XA_HEADS * XA_HEAD_DIM), D_MODEL ** -0.5)
    xa_wk = normal(ks[26], (L, D_MODEL, XA_HEADS * XA_HEAD_DIM), D_MODEL ** -0.5)
    xa_wv = normal(ks[27], (L, D_MODEL, XA_HEADS * XA_HEAD_DIM), D_MODEL ** -0.5 * DEEPNORM_BETA)
    xa_wo = normal(ks[28], (L, XA_HEADS * XA_HEAD_DIM, D_MODEL), D_MODEL ** -0.5 * DEEPNORM_BETA)
    ln_xa_g = 1.0 + normal(ks[29], (L, D_MODEL), 0.02)
    ln_xa_b = normal(ks[30], (L, D_MODEL), 0.02)
    mlp_w1 = normal(ks[31], (L, D_MODEL, D_FF), D_MODEL ** -0.5 * DEEPNORM_BETA)
    mlp_w2 = normal(ks[32], (L, D_FF, D_MODEL), D_FF ** -0.5 * DEEPNORM_BETA)
    ln_mlp_g = 1.0 + normal(ks[33], (L, D_MODEL), 0.02)
    ln_mlp_b = normal(ks[34], (L, D_MODEL), 0.02)
    return {
        'x': x, 'mem': mem, 'ln_in_g': ln_in_g, 'ln_in_b': ln_in_b, 'w_in': w_in, 'na_rpb': na_rpb,
        'ssd_conv_w': ssd_conv_w, 'ssd_conv_b': ssd_conv_b, 'ssd_dt_bias': ssd_dt_bias,
        'ssd_a_log': ssd_a_log, 'ssd_d': ssd_d, 'ssd_norm_w': ssd_norm_w,
        's5_lam_re': s5_lam_re, 's5_lam_im': s5_lam_im, 's5_log_dt': s5_log_dt,
        's5_b_re': s5_b_re, 's5_b_im': s5_b_im, 's5_c_re': s5_c_re, 's5_c_im': s5_c_im,
        's5_d': s5_d, 's5_glu_w': s5_glu_w, 's5_glu_b': s5_glu_b, 'w_mix_out': w_mix_out,
        'ln_mix_g': ln_mix_g, 'ln_mix_b': ln_mix_b, 'xa_wq': xa_wq, 'xa_wk': xa_wk,
        'xa_wv': xa_wv, 'xa_wo': xa_wo, 'ln_xa_g': ln_xa_g, 'ln_xa_b': ln_xa_b,
        'mlp_w1': mlp_w1, 'mlp_w2': mlp_w2, 'ln_mlp_g': ln_mlp_g, 'ln_mlp_b': ln_mlp_b,
    }


def reference(x, mem, ln_in_g, ln_in_b, w_in, na_rpb, ssd_conv_w, ssd_conv_b, ssd_dt_bias,
              ssd_a_log, ssd_d, ssd_norm_w, s5_lam_re, s5_lam_im, s5_log_dt, s5_b_re, s5_b_im,
              s5_c_re, s5_c_im, s5_d, s5_glu_w, s5_glu_b, w_mix_out, ln_mix_g, ln_mix_b,
              xa_wq, xa_wk, xa_wv, xa_wo, ln_xa_g, ln_xa_b, mlp_w1, mlp_w2, ln_mlp_g, ln_mlp_b):
    h = layernorm(x, ln_in_g, ln_in_b)
    for l in range(DEPTH):
        mix = hybrid_mixer(h, w_in[l], na_rpb[l], ssd_conv_w[l], ssd_conv_b[l], ssd_dt_bias[l],
                           ssd_a_log[l], ssd_d[l], ssd_norm_w[l], s5_lam_re[l], s5_lam_im[l],
                           s5_log_dt[l], s5_b_re[l], s5_b_im[l], s5_c_re[l], s5_c_im[l], s5_d[l],
                           s5_glu_w[l], s5_glu_b[l], w_mix_out[l])
        h = layernorm(DEEPNORM_ALPHA * h + mix, ln_mix_g[l], ln_mix_b[l])
        xa = cross_attention(h, mem, xa_wq[l], xa_wk[l], xa_wv[l], xa_wo[l])
        h = layernorm(DEEPNORM_ALPHA * h + xa, ln_xa_g[l], ln_xa_b[l])
        ff = squared_relu_mlp(h, mlp_w1[l], mlp_w2[l])
        h = layernorm(DEEPNORM_ALPHA * h + ff, ln_mlp_g[l], ln_mlp_b[l])
    return h
```

```python
import functools
import math

import jax
import jax.numpy as jnp
import numpy as np
from jax import lax
from jax.experimental import pallas as pl
from jax.experimental.pallas import tpu as pltpu

F32 = jnp.float32
BF16 = jnp.bfloat16
HIGHEST = lax.Precision.HIGHEST

D_MODEL = 1024
DEPTH = 2
GRID_W = 64
HEAD_DIM = 64
NA_DIM = 256
NA_HEADS = 4
NA_WIN_ROWS = 8
NA_WIN_COLS = 16
SSD_DIM = 512
SSD_HEADS = 8
SSD_GROUPS = 2
SSD_STATE = 64
SSD_CONV = 5
SSD_CHUNK = 128
SSD_CONV_DIM = SSD_DIM + 2 * SSD_GROUPS * SSD_STATE
S5_DIM = 256
S5_GROUP_CH = 16
S5_GROUPS = 16
S5_STATE = 64
XA_HEADS = 4
XA_HEAD_DIM = 256
D_FF = 4096
LN_EPS = 1e-5
NEG_BIG = -1e30
DEEPNORM_ALPHA = (2 * DEPTH) ** 0.25

ROW_TILE = 512
NA_Q_ROWS = 4
NA_K_ROWS = NA_Q_ROWS + NA_WIN_ROWS
SSD_BLOCK_CHUNKS = 4
S5_CHUNK = 16
HALO = 8
DT_PAD = 128
VMEM_LIMIT = 56 * 1024 * 1024


def _params(*sem):
    return pltpu.CompilerParams(dimension_semantics=sem, vmem_limit_bytes=VMEM_LIMIT)


def _layernorm(x, g, b):
    mu = jnp.mean(x, axis=-1, keepdims=True)
    xc = x - mu
    var = jnp.mean(xc * xc, axis=-1, keepdims=True)
    return xc * lax.rsqrt(var + LN_EPS) * g + b


def _silu(x):
    return x / (1.0 + jnp.exp(-x))


def _softplus(x):
    return jnp.maximum(x, 0.0) + jnp.log1p(jnp.exp(-jnp.abs(x)))


def _const_spec(shape):
    n = len(shape)
    return pl.BlockSpec(shape, lambda *_: (0,) * n)


IN_SPLITS = ((0, 768), (768, 1280), (1280, 2048), (2048, 2304), (2304, 2304 + DT_PAD))
IN_DTYPES = (BF16, F32, F32, F32, F32)


def _inproj_kernel(x_ref, g_ref, b_ref, w_ref, *out_refs, apply_ln):
    x = x_ref[...]
    if apply_ln:
        x = _layernorm(x, g_ref[...], b_ref[...])
        out_refs[5][...] = x
    xb = x.astype(BF16)
    for ref, (lo, hi) in zip(out_refs[:5], IN_SPLITS):
        ref[...] = jnp.dot(xb, w_ref[:, lo:hi], preferred_element_type=F32).astype(ref.dtype)


def _inproj(x, g, b, w, apply_ln):
    n = x.shape[0]
    widths = [hi - lo for lo, hi in IN_SPLITS]
    out_shape = [jax.ShapeDtypeStruct((n, wd), dt) for wd, dt in zip(widths, IN_DTYPES)]
    out_specs = [pl.BlockSpec((ROW_TILE, wd), lambda i: (i, 0)) for wd in widths]
    if apply_ln:
        out_shape.append(jax.ShapeDtypeStruct((n, D_MODEL), F32))
        out_specs.append(pl.BlockSpec((ROW_TILE, D_MODEL), lambda i: (i, 0)))
    return pl.pallas_call(
        functools.partial(_inproj_kernel, apply_ln=apply_ln),
        grid=(n // ROW_TILE,),
        in_specs=[pl.BlockSpec((ROW_TILE, D_MODEL), lambda i: (i, 0)),
                  _const_spec((1, D_MODEL)), _const_spec((1, D_MODEL)),
                  _const_spec(w.shape)],
        out_specs=out_specs, out_shape=out_shape,
        compiler_params=_params("parallel"), name="in_proj",
    )(x, g, b, w)


def _na_bias_tables(rpb, rows):
    qr = jnp.arange(NA_Q_ROWS)
    kr = jnp.arange(NA_K_ROWS)
    c = jnp.arange(GRID_W)
    c0 = jnp.clip(c - NA_WIN_COLS // 2, 0, GRID_W - NA_WIN_COLS)
    col_ok = (c[None, :] >= c0[:, None]) & (c[None, :] < c0[:, None] + NA_WIN_COLS)
    ci = jnp.clip(c[None, :] - c[:, None], -(NA_WIN_COLS - 1), NA_WIN_COLS - 1) + (NA_WIN_COLS - 1)
    tables = []
    for blk_row in (0, NA_Q_ROWS, rows - NA_Q_ROWS):
        start = min(max(blk_row - NA_WIN_ROWS // 2, 0), rows - NA_K_ROWS)
        r = blk_row + qr
        r0 = jnp.clip(r - NA_WIN_ROWS // 2, 0, rows - NA_WIN_ROWS)
        key_row = start + kr
        row_ok = (key_row[None, :] >= r0[:, None]) & (key_row[None, :] < r0[:, None] + NA_WIN_ROWS)
        ri = jnp.clip(key_row[None, :] - r[:, None] + (NA_WIN_ROWS - 1), 0, 2 * NA_WIN_ROWS - 2)
        bias = rpb[:, ri[:, None, :, None], ci[None, :, None, :]]
        ok = row_ok[:, None, :, None] & col_ok[None, :, None, :]
        bias = jnp.where(ok[None], bias, NEG_BIG)
        tables.append(bias.reshape(NA_HEADS, NA_Q_ROWS * GRID_W, NA_K_ROWS * GRID_W))
    return jnp.stack(tables).astype(F32)


def _na_kernel(q_ref, k_ref, v_ref, bias_ref, o_ref, *, rows):
    i = pl.program_id(1)
    nblk = rows // NA_Q_ROWS
    case = jnp.where(i == 0, 0, jnp.where(i == nblk - 1, 2, 1))
    start_row = jnp.clip(i * NA_Q_ROWS - NA_WIN_ROWS // 2, 0, rows - NA_K_ROWS)
    start = pl.multiple_of(start_row * GRID_W, GRID_W)
    nk = NA_K_ROWS * GRID_W
    q = q_ref[...] * (HEAD_DIM ** -0.5)
    kw = k_ref[pl.ds(start, nk), :]
    vw = v_ref[pl.ds(start, nk), :]
    for h in range(NA_HEADS):
        sl = slice(h * HEAD_DIM, (h + 1) * HEAD_DIM)
        s = lax.dot_general(q[:, sl], kw[:, sl], (((1,), (1,)), ((), ())), preferred_element_type=F32)
        s = s + bias_ref[case, h]
        m = jnp.max(s, axis=-1, keepdims=True)
        p = jnp.exp(s - m)
        l = jnp.sum(p, axis=-1, keepdims=True)
        o = jnp.dot(p.astype(BF16), vw[:, sl], preferred_element_type=F32)
        o_ref[:, sl] = (o / l).astype(o_ref.dtype)


def _na(qkv, bias, bsz, t):
    rows = t // GRID_W
    tq = NA_Q_ROWS * GRID_W
    qkv3 = qkv.reshape(bsz, t, 3 * NA_DIM)
    out = pl.pallas_call(
        functools.partial(_na_kernel, rows=rows),
        grid=(bsz, rows // NA_Q_ROWS),
        in_specs=[pl.BlockSpec((None, tq, NA_DIM), lambda b, i: (b, i, 0)),
                  pl.BlockSpec((None, t, NA_DIM), lambda b, i: (b, 0, 1)),
                  pl.BlockSpec((None, t, NA_DIM), lambda b, i: (b, 0, 2)),
                  pl.BlockSpec(bias.shape, lambda b, i: (0, 0, 0, 0), pipeline_mode=pl.Buffered(1))],
        out_specs=pl.BlockSpec((None, tq, NA_DIM), lambda b, i: (b, i, 0)),
        out_shape=jax.ShapeDtypeStruct((bsz, t, NA_DIM), BF16),
        compiler_params=_params("parallel", "arbitrary"), name="na_attn",
    )(qkv3, qkv3, qkv3, bias)
    return out.reshape(bsz * t, NA_DIM)


def _conv_silu(xm_ref, xp_ref, xn_ref, cw_ref, cb_ref, ext_ref, blk, nblk, ncols):
    tb = xm_ref.shape[0]
    pad = SSD_CONV // 2
    ext_ref[0:HALO, :] = jnp.where(blk == 0, 0.0, xp_ref[...])
    ext_ref[HALO:HALO + tb, :] = xm_ref[...]
    ext_ref[HALO + tb:HALO + tb + HALO, :] = jnp.where(blk == nblk - 1, 0.0, xn_ref[...])
    acc = cb_ref[:, :ncols]
    for k in range(SSD_CONV):
        acc = acc + ext_ref[HALO - pad + k:HALO - pad + k + tb, :ncols] * cw_ref[k:k + 1, :ncols]
    return _silu(acc)


def _tri(n, lower):
    r = lax.broadcasted_iota(jnp.int32, (n, n), 0)
    c = lax.broadcasted_iota(jnp.int32, (n, n), 1)
    return (r >= c) if lower else (r <= c)


def _head_expand():
    r = lax.broadcasted_iota(jnp.int32, (SSD_HEADS, SSD_DIM), 0)
    c = lax.broadcasted_iota(jnp.int32, (SSD_HEADS, SSD_DIM), 1)
    return (c // HEAD_DIM == r).astype(F32)


def _ssd_state_kernel(xm_ref, xp_ref, xn_ref, dt_ref, cw_ref, cb_ref, dtb_ref, alog_ref,
                      prev_ref, ext_ref, state_ref, *, backward, nblk):
    j = pl.program_id(1)
    blk = (nblk - 1 - j) if backward else j
    q = SSD_CHUNK

    @pl.when(j == 0)
    def _():
        state_ref[...] = jnp.zeros_like(state_ref)

    ncols = SSD_DIM + SSD_GROUPS * SSD_STATE
    xc = _conv_silu(xm_ref, xp_ref, xn_ref, cw_ref, cb_ref, ext_ref, blk, nblk, ncols)
    a = -jnp.exp(alog_ref[...])
    dt = _softplus(dt_ref[...] + dtb_ref[...])
    tri = _tri(q, lower=not backward).astype(F32)
    expand = _head_expand()
    order = range(SSD_BLOCK_CHUNKS - 1, -1, -1) if backward else range(SSD_BLOCK_CHUNKS)
    for cc in order:
        rs = slice(cc * q, (cc + 1) * q)
        dtc = dt[rs]
        da = dtc * a
        cs = jnp.dot(tri, da, precision=HIGHEST, preferred_element_type=F32)
        tot = jnp.sum(da, axis=0, keepdims=True)
        w = jnp.exp(tot - cs) * dtc
        xw = (xc[rs, :SSD_DIM] * jnp.dot(w, expand, precision=HIGHEST, preferred_element_type=F32)).astype(BF16)
        chunk_decay = jnp.dot(jnp.broadcast_to(jnp.exp(tot), (8, SSD_HEADS)), expand,
                              precision=HIGHEST, preferred_element_type=F32)[0:1]
        prev_ref[cc] = state_ref[...]
        for g in range(SSD_GROUPS):
            bg = xc[rs, SSD_DIM + g * SSD_STATE:SSD_DIM + (g + 1) * SSD_STATE].astype(BF16)
            cols = slice(g * 256, (g + 1) * 256)
            s_g = lax.dot_general(bg, xw[:, cols], (((0,), (0,)), ((), ())), preferred_element_type=F32)
            state_ref[:, cols] = state_ref[:, cols] * chunk_decay[:, cols] + s_g


def _ssd_states(xbc3, dt4, conv_w, conv_b, dt_bias, a_log, backward):
    bsz, t, _ = xbc3.shape
    tb = SSD_BLOCK_CHUNKS * SSD_CHUNK
    nblk = t // tb
    hb = tb // HALO
    d = 1 if backward else 0
    eff = (lambda j: nblk - 1 - j) if backward else (lambda j: j)
    return pl.pallas_call(
        functools.partial(_ssd_state_kernel, backward=backward, nblk=nblk),
        grid=(bsz, nblk),
        in_specs=[pl.BlockSpec((None, tb, SSD_CONV_DIM), lambda b, j: (b, eff(j), 0)),
                  pl.BlockSpec((None, HALO, SSD_CONV_DIM), lambda b, j: (b, jnp.maximum(eff(j) * hb - 1, 0), 0)),
                  pl.BlockSpec((None, HALO, SSD_CONV_DIM),
                               lambda b, j: (b, jnp.minimum((eff(j) + 1) * hb, t // HALO - 1), 0)),
                  pl.BlockSpec((None, None, tb, SSD_HEADS), lambda b, j: (d, b, eff(j), 0)),
                  _const_spec((SSD_CONV, SSD_CONV_DIM)), _const_spec((1, SSD_CONV_DIM)),
                  pl.BlockSpec((None, 1, SSD_HEADS), lambda b, j: (d, 0, 0)),
                  pl.BlockSpec((None, 1, SSD_HEADS), lambda b, j: (d, 0, 0))],
        out_specs=pl.BlockSpec((None, SSD_BLOCK_CHUNKS, SSD_STATE, SSD_DIM), lambda b, j: (b, eff(j), 0, 0)),
        out_shape=jax.ShapeDtypeStruct((bsz, t // SSD_CHUNK, SSD_STATE, SSD_DIM), F32),
        scratch_shapes=[pltpu.VMEM((tb + 2 * HALO, SSD_CONV_DIM), F32),
                        pltpu.VMEM((SSD_STATE, SSD_DIM), F32)],
        compiler_params=_params("parallel", "arbitrary"),
        name="ssd_state_bwd" if backward else "ssd_state_fwd",
    )(xbc3, xbc3, xbc3, dt4, conv_w, conv_b, dt_bias, a_log)


def _ssd_out_kernel(xm_ref, xp_ref, xn_ref, z_ref, dtf_ref, dtb_ref, dtft_ref, dtbt_ref, pf_ref, pb_ref,
                    cw_ref, cb_ref, bias_row_ref, bias_col_ref, alog_row_ref, alog_col_ref,
                    dskip_ref, nw_ref, o_ref, ext_ref, y_ref, *, nblk):
    blk = pl.program_id(1)
    q = SSD_CHUNK
    xc = _conv_silu(xm_ref, xp_ref, xn_ref, cw_ref, cb_ref, ext_ref, blk, nblk, SSD_CONV_DIM)
    expand = _head_expand()
    lower = _tri(q, True)
    upper = _tri(q, False)
    a_row = -jnp.exp(alog_row_ref[...])
    a_col = -jnp.exp(alog_col_ref[...])
    dt_col = (_softplus(dtf_ref[...] + bias_row_ref[0]), _softplus(dtb_ref[...] + bias_row_ref[1]))
    dt_row = (_softplus(dtft_ref[...] + bias_col_ref[0]), _softplus(dtbt_ref[...] + bias_col_ref[1]))
    prev = (pf_ref, pb_ref)
    masks = (lower, upper)
    for cc in range(SSD_BLOCK_CHUNKS):
        rs = slice(cc * q, (cc + 1) * q)
        xs = xc[rs, :SSD_DIM]
        xsb = xs.astype(BF16)
        y = dskip_ref[...] * xs
        cs_col, cs_row, dtr = [], [], []
        for d in range(2):
            tri = masks[d].astype(F32)
            da_col = dt_col[d][rs] * a_row[d]
            dtr_d = dt_row[d][:, rs]
            da_row = dtr_d * a_col[d]
            cs_col.append(jnp.dot(tri, da_col, precision=HIGHEST, preferred_element_type=F32))
            cs_row.append(lax.dot_general(da_row, tri, (((1,), (1,)), ((), ())), precision=HIGHEST,
                                          preferred_element_type=F32))
            dtr.append(dtr_d)
        for g in range(SSD_GROUPS):
            off = SSD_DIM + g * SSD_STATE
            bg = xc[rs, off:off + SSD_STATE].astype(BF16)
            cg = xc[rs, off + SSD_GROUPS * SSD_STATE:off + SSD_GROUPS * SSD_STATE + SSD_STATE].astype(BF16)
            cb = lax.dot_general(cg, bg, (((1,), (1,)), ((), ())), preferred_element_type=F32)
            cols = slice(g * 256, (g + 1) * 256)
            y_off = None
            for d in range(2):
                e = jnp.dot(jnp.exp(cs_col[d]), expand[:, cols], precision=HIGHEST, preferred_element_type=F32)
                t_d = jnp.dot(cg, prev[d][cc, :, cols].astype(BF16), preferred_element_type=F32) * e
                y_off = t_d if y_off is None else y_off + t_d
            y_ref[:, cols] = y[:, cols] + y_off
            for hh in range(SSD_HEADS // SSD_GROUPS):
                h = g * (SSD_HEADS // SSD_GROUPS) + hh
                lmat = None
                for d in range(2):
                    seg = cs_col[d][:, h:h + 1] - cs_row[d][h:h + 1, :]
                    l_d = jnp.exp(jnp.where(masks[d], seg, NEG_BIG)) * dtr[d][h:h + 1, :]
                    lmat = l_d if lmat is None else lmat + l_d
                hs = slice(h * HEAD_DIM, (h + 1) * HEAD_DIM)
                y_ref[:, hs] += jnp.dot((cb * lmat).astype(BF16), xsb[:, hs], preferred_element_type=F32)
        yg = y_ref[...] * _silu(z_ref[rs, :])
        ms = jnp.mean(yg * yg, axis=-1, keepdims=True)
        o_ref[rs, :] = (yg * lax.rsqrt(ms + LN_EPS) * nw_ref[...]).astype(o_ref.dtype)


def _ssd_out(xbc3, z3, dt4, dtt4, prev_f, prev_b, conv_w, conv_b, dt_bias, a_log, d_skip, norm_w):
    bsz, t, _ = xbc3.shape
    tb = SSD_BLOCK_CHUNKS * SSD_CHUNK
    nblk = t // tb
    hb = tb // HALO
    out = pl.pallas_call(
        functools.partial(_ssd_out_kernel, nblk=nblk),
        grid=(bsz, nblk),
        in_specs=[pl.BlockSpec((None, tb, SSD_CONV_DIM), lambda b, j: (b, j, 0)),
                  pl.BlockSpec((None, HALO, SSD_CONV_DIM), lambda b, j: (b, jnp.maximum(j * hb - 1, 0), 0)),
                  pl.BlockSpec((None, HALO, SSD_CONV_DIM),
                               lambda b, j: (b, jnp.minimum((j + 1) * hb, t // HALO - 1), 0)),
                  pl.BlockSpec((None, tb, SSD_DIM), lambda b, j: (b, j, 0)),
                  pl.BlockSpec((None, None, tb, SSD_HEADS), lambda b, j: (0, b, j, 0)),
                  pl.BlockSpec((None, None, tb, SSD_HEADS), lambda b, j: (1, b, j, 0)),
                  pl.BlockSpec((None, None, SSD_HEADS, tb), lambda b, j: (0, b, 0, j)),
                  pl.BlockSpec((None, None, SSD_HEADS, tb), lambda b, j: (1, b, 0, j)),
                  pl.BlockSpec((None, SSD_BLOCK_CHUNKS, SSD_STATE, SSD_DIM), lambda b, j: (b, j, 0, 0)),
                  pl.BlockSpec((None, SSD_BLOCK_CHUNKS, SSD_STATE, SSD_DIM), lambda b, j: (b, j, 0, 0)),
                  _const_spec((SSD_CONV, SSD_CONV_DIM)), _const_spec((1, SSD_CONV_DIM)),
                  _const_spec((2, 1, SSD_HEADS)), _const_spec((2, SSD_HEADS, 1)),
                  _const_spec((2, 1, SSD_HEADS)), _const_spec((2, SSD_HEADS, 1)),
                  _const_spec((1, SSD_DIM)), _const_spec((1, SSD_DIM))],
        out_specs=pl.BlockSpec((None, tb, SSD_DIM), lambda b, j: (b, j, 0)),
        out_shape=jax.ShapeDtypeStruct((bsz, t, SSD_DIM), BF16),
        scratch_shapes=[pltpu.VMEM((tb + 2 * HALO, SSD_CONV_DIM), F32),
                        pltpu.VMEM((SSD_CHUNK, SSD_DIM), F32)],
        compiler_params=_params("parallel", "arbitrary"), name="ssd_out",
    )(xbc3, xbc3, xbc3, z3, dt4, dt4, dtt4, dtt4, prev_f, prev_b, conv_w, conv_b,
      dt_bias.reshape(2, 1, SSD_HEADS), dt_bias.reshape(2, SSD_HEADS, 1),
      a_log.reshape(2, 1, SSD_HEADS), a_log.reshape(2, SSD_HEADS, 1), d_skip, norm_w)
    return out.reshape(bsz * t, SSD_DIM)


def _s5_tables(lam_re, lam_im, log_dt, b_re, b_im, c_re, c_im):
    lc = S5_CHUNK
    hp = HIGHEST
    pw_re, pw_im, bb_re, bb_im = [], [], [], []
    for d in range(2):
        dt = jnp.exp(log_dt[d])[:, None]
        lr, li = lam_re[d], lam_im[d]
        mag = jnp.exp(lr * dt)
        ar, ai = mag * jnp.cos(li * dt), mag * jnp.sin(li * dt)
        den = lr * lr + li * li
        fr = ((ar - 1.0) * lr + ai * li) / den
        fi = (ai * lr - (ar - 1.0) * li) / den
        bb_re.append(fr[..., None] * b_re[d] - fi[..., None] * b_im[d])
        bb_im.append(fr[..., None] * b_im[d] + fi[..., None] * b_re[d])
        pr, pi = [jnp.ones_like(ar)], [jnp.zeros_like(ar)]
        for _ in range(lc):
            pr, pi = pr + [pr[-1] * ar - pi[-1] * ai], pi + [pr[-1] * ai + pi[-1] * ar]
        pw_re.append(jnp.stack(pr))
        pw_im.append(jnp.stack(pi))

    def kern(d):
        wr = pw_re[d][:lc, :, :, None] * bb_re[d][None] - pw_im[d][:lc, :, :, None] * bb_im[d][None]
        wi = pw_re[d][:lc, :, :, None] * bb_im[d][None] + pw_im[d][:lc, :, :, None] * bb_re[d][None]
        return (jnp.einsum('ghp,kgpj->kghj', c_re[d], wr, precision=hp)
                - jnp.einsum('ghp,kgpj->kghj', c_im[d], wi, precision=hp))

    kf, kb = kern(0), kern(1)
    l = jnp.arange(lc)
    lag = l[:, None] - l[None, :]
    tf = jnp.where((lag >= 0)[:, :, None, None, None], kf[jnp.clip(lag, 0, lc - 1)], 0.0)
    tb = jnp.where((lag <= 0)[:, :, None, None, None], kb[jnp.clip(-lag, 0, lc - 1)], 0.0)
    toep = (tf + tb).transpose(2, 1, 4, 0, 3).reshape(S5_GROUPS, lc * S5_GROUP_CH, lc * S5_GROUP_CH)

    def state_in(d, powers):
        wr = pw_re[d][powers][:, :, :, None] * bb_re[d][None] - pw_im[d][powers][:, :, :, None] * bb_im[d][None]
        wi = pw_re[d][powers][:, :, :, None] * bb_im[d][None] + pw_im[d][powers][:, :, :, None] * bb_re[d][None]
        to_rows = lambda w: w.transpose(1, 0, 3, 2).reshape(S5_GROUPS, lc * S5_GROUP_CH, S5_STATE)
        return to_rows(wr), to_rows(wi)

    f_re, f_im = state_in(0, lc - 1 - l)
    b_re_, b_im_ = state_in(1, l)
    m_state = jnp.concatenate([f_re, b_re_, f_im, b_im_], axis=-1)

    def state_out(d, powers):
        cpr = c_re[d][None] * pw_re[d][powers][:, :, None, :] - c_im[d][None] * pw_im[d][powers][:, :, None, :]
        cpi = c_re[d][None] * pw_im[d][powers][:, :, None, :] + c_im[d][None] * pw_re[d][powers][:, :, None, :]
        to_cols = lambda w: w.transpose(1, 3, 0, 2).reshape(S5_GROUPS, S5_STATE, lc * S5_GROUP_CH)
        return to_cols(cpr), -to_cols(cpi)

    of_re, of_im = state_out(0, l + 1)
    ob_re, ob_im = state_out(1, lc - l)
    m_off = jnp.concatenate([of_re, ob_re, of_im, ob_im], axis=1)
    dec_re = jnp.concatenate([pw_re[0][lc], pw_re[1][lc]], axis=-1)[:, None, :]
    dec_im = jnp.concatenate([pw_im[0][lc], pw_im[1][lc]], axis=-1)[:, None, :]
    return toep.astype(BF16), m_state.astype(BF16), m_off.astype(BF16), dec_re, dec_im


def _s5_kernel(u_ref, toep_ref, mst_ref, moff_ref, dtile_ref, are_ref, aim_ref, y_ref, s_ref, e_ref, *, nc, bsz):
    u = u_ref[...]
    ub = u.astype(BF16)
    y_ref[...] = jnp.dot(ub, toep_ref[...], preferred_element_type=F32) + dtile_ref[...] * u
    s_ref[...] = jnp.dot(ub, mst_ref[...], preferred_element_type=F32)
    half = S5_STATE
    ar = jnp.broadcast_to(are_ref[...], (bsz, 2 * half))
    ai = jnp.broadcast_to(aim_ref[...], (bsz, 2 * half))
    is_fwd = lax.broadcasted_iota(jnp.int32, (bsz, 2 * half), 1) < half

    def body(i, carry):
        er, ei = carry
        rf = pl.multiple_of(i * bsz, bsz)
        rb = pl.multiple_of((nc - 1 - i) * bsz, bsz)
        e_ref[pl.ds(rf, bsz), 0:half] = er[:, :half]
        e_ref[pl.ds(rb, bsz), half:2 * half] = er[:, half:]
        e_ref[pl.ds(rf, bsz), 2 * half:3 * half] = ei[:, :half]
        e_ref[pl.ds(rb, bsz), 3 * half:4 * half] = ei[:, half:]
        sf = s_ref[pl.ds(rf, bsz), :]
        sb = s_ref[pl.ds(rb, bsz), :]
        sr = jnp.where(is_fwd, sf[:, :2 * half], sb[:, :2 * half])
        si = jnp.where(is_fwd, sf[:, 2 * half:], sb[:, 2 * half:])
        return ar * er - ai * ei + sr, ar * ei + ai * er + si

    zero = jnp.zeros((bsz, 2 * half), F32)
    lax.fori_loop(0, nc, body, (zero, zero))
    y_ref[...] += jnp.dot(e_ref[...].astype(BF16), moff_ref[...], preferred_element_type=F32)


def _s5(u, tables, d_skip, bsz, t):
    toep, m_state, m_off, dec_re, dec_im = tables
    lc = S5_CHUNK
    nc = t // lc
    w = lc * S5_GROUP_CH
    ug = u.reshape(bsz, nc, lc, S5_GROUPS, S5_GROUP_CH).transpose(3, 1, 0, 2, 4).reshape(S5_GROUPS, nc * bsz, w)
    dtile = jnp.tile(d_skip.reshape(S5_GROUPS, 1, S5_GROUP_CH), (1, 1, lc))
    grp = lambda shape: pl.BlockSpec((None,) + shape, lambda g: (g, 0, 0))
    y = pl.pallas_call(
        functools.partial(_s5_kernel, nc=nc, bsz=bsz),
        grid=(S5_GROUPS,),
        in_specs=[grp((nc * bsz, w)), grp((w, w)), grp((w, w)), grp((w, w)), grp((1, w)),
                  grp((1, 2 * S5_STATE)), grp((1, 2 * S5_STATE))],
        out_specs=grp((nc * bsz, w)),
        out_shape=jax.ShapeDtypeStruct((S5_GROUPS, nc * bsz, w), F32),
        scratch_shapes=[pltpu.VMEM((nc * bsz, w), F32), pltpu.VMEM((nc * bsz, w), F32)],
        compiler_params=_params("parallel"), name="s5_scan",
    )(ug, toep, m_state, m_off, dtile, dec_re, dec_im)
    y = y.reshape(S5_GROUPS, nc, bsz, lc, S5_GROUP_CH).transpose(2, 1, 3, 0, 4)
    return y.reshape(bsz * t, S5_DIM)


def _mixout_kernel(h_ref, na_ref, ssd_ref, y5_ref, gw_ref, gb_ref, wo_ref, g_ref, b_ref, o_ref):
    y5 = y5_ref[...]
    gl = 0.5 * y5 * (1.0 + jnp.tanh(math.sqrt(2.0 / math.pi) * (y5 + 0.044715 * (y5 * y5 * y5))))
    gate = jnp.dot(gl.astype(BF16), gw_ref[...], preferred_element_type=F32) + gb_ref[...]
    o5 = gl / (1.0 + jnp.exp(-gate))
    mix = jnp.dot(na_ref[...], wo_ref[0:NA_DIM, :], preferred_element_type=F32)
    mix += jnp.dot(ssd_ref[...], wo_ref[NA_DIM:NA_DIM + SSD_DIM, :], preferred_element_type=F32)
    mix += jnp.dot(o5.astype(BF16), wo_ref[NA_DIM + SSD_DIM:, :], preferred_element_type=F32)
    o_ref[...] = _layernorm(DEEPNORM_ALPHA * h_ref[...] + mix, g_ref[...], b_ref[...])


def _mixout(h, o_na, o_ssd, y5, glu_w, glu_b, w_out, g, b):
    n = h.shape[0]
    row = lambda wd: pl.BlockSpec((ROW_TILE, wd), lambda i: (i, 0))
    return pl.pallas_call(
        _mixout_kernel, grid=(n // ROW_TILE,),
        in_specs=[row(D_MODEL), row(NA_DIM), row(SSD_DIM), row(S5_DIM),
                  _const_spec((S5_DIM, S5_DIM)), _const_spec((1, S5_DIM)),
                  _const_spec((D_MODEL, D_MODEL)), _const_spec((1, D_MODEL)), _const_spec((1, D_MODEL))],
        out_specs=row(D_MODEL), out_shape=jax.ShapeDtypeStruct((n, D_MODEL), F32),
        compiler_params=_params("parallel"), name="mix_out",
    )(h, o_na, o_ssd, y5, glu_w, glu_b, w_out, g, b)


def _kvproj_kernel(m_ref, wk_ref, wv_ref, k_ref, v_ref):
    mb = m_ref[...].astype(BF16)
    k_ref[...] = jnp.dot(mb, wk_ref[...], preferred_element_type=F32).astype(k_ref.dtype)
    v_ref[...] = jnp.dot(mb, wv_ref[...], preferred_element_type=F32).astype(v_ref.dtype)


def _kvproj(mem2, wk, wv):
    n = mem2.shape[0]
    tm = min(ROW_TILE, n)
    row = pl.BlockSpec((tm, D_MODEL), lambda i: (i, 0))
    return pl.pallas_call(
        _kvproj_kernel, grid=(n // tm,),
        in_specs=[row, _const_spec((D_MODEL, D_MODEL)), _const_spec((D_MODEL, D_MODEL))],
        out_specs=[row, row], out_shape=[jax.ShapeDtypeStruct((n, D_MODEL), BF16)] * 2,
        compiler_params=_params("parallel"), name="xa_kv_proj",
    )(mem2, wk, wv)


def _xattn_kernel(h_ref, k_ref, v_ref, wq_ref, wo_ref, g_ref, b_ref, o_ref):
    h = h_ref[...]
    q = jnp.dot(h.astype(BF16), wq_ref[...], preferred_element_type=F32) * (XA_HEAD_DIM ** -0.5)
    qb = q.astype(BF16)
    xa = None
    for hd in range(XA_HEADS):
        sl = slice(hd * XA_HEAD_DIM, (hd + 1) * XA_HEAD_DIM)
        s = lax.dot_general(qb[:, sl], k_ref[:, sl], (((1,), (1,)), ((), ())), preferred_element_type=F32)
        m = jnp.max(s, axis=-1, keepdims=True)
        p = jnp.exp(s - m)
        l = jnp.sum(p, axis=-1, keepdims=True)
        o = jnp.dot(p.astype(BF16), v_ref[:, sl], preferred_element_type=F32) / l
        part = jnp.dot(o.astype(BF16), wo_ref[sl, :], preferred_element_type=F32)
        xa = part if xa is None else xa + part
    o_ref[...] = _layernorm(DEEPNORM_ALPHA * h + xa, g_ref[...], b_ref[...])


def _xattn(h, k3, v3, wq, wo, g, b, bsz, t):
    m = k3.shape[1]
    h3 = h.reshape(bsz, t, D_MODEL)
    row = pl.BlockSpec((None, ROW_TILE, D_MODEL), lambda bb, i: (bb, i, 0))
    kv = pl.BlockSpec((None, m, D_MODEL), lambda bb, i: (bb, 0, 0))
    const = lambda shape: pl.BlockSpec(shape, lambda bb, i: (0, 0))
    out = pl.pallas_call(
        _xattn_kernel, grid=(bsz, t // ROW_TILE),
        in_specs=[row, kv, kv, const((D_MODEL, D_MODEL)), const((D_MODEL, D_MODEL)),
                  const((1, D_MODEL)), const((1, D_MODEL))],
        out_specs=row, out_shape=jax.ShapeDtypeStruct((bsz, t, D_MODEL), F32),
        compiler_params=_params("parallel", "parallel"), name="cross_attn",
    )(h3, k3, v3, wq, wo, g, b)
    return out.reshape(bsz * t, D_MODEL)


FF_TILE = 1024


def _mlp_kernel(h_ref, w1_ref, w2_ref, g_ref, b_ref, o_ref):
    h = h_ref[...]
    hb = h.astype(BF16)
    ff = None
    for c in range(D_FF // FF_TILE):
        sl = slice(c * FF_TILE, (c + 1) * FF_TILE)
        a = jnp.maximum(jnp.dot(hb, w1_ref[:, sl], preferred_element_type=F32), 0.0)
        part = jnp.dot((a * a).astype(BF16), w2_ref[sl, :], preferred_element_type=F32)
        ff = part if ff is None else ff + part
    o_ref[...] = _layernorm(DEEPNORM_ALPHA * h + ff, g_ref[...], b_ref[...])


def _mlp(h, w1, w2, g, b):
    n = h.shape[0]
    row = pl.BlockSpec((ROW_TILE, D_MODEL), lambda i: (i, 0))
    return pl.pallas_call(
        _mlp_kernel, grid=(n // ROW_TILE,),
        in_specs=[row,
                  pl.BlockSpec((D_MODEL, D_FF), lambda i: (0, 0), pipeline_mode=pl.Buffered(1)),
                  pl.BlockSpec((D_FF, D_MODEL), lambda i: (0, 0), pipeline_mode=pl.Buffered(1)),
                  _const_spec((1, D_MODEL)), _const_spec((1, D_MODEL))],
        out_specs=row, out_shape=jax.ShapeDtypeStruct((n, D_MODEL), F32),
        compiler_params=_params("parallel"), name="mlp",
    )(h, w1, w2, g, b)


def _permute_w_in(w):
    dt_lo = 3 * NA_DIM + SSD_DIM + SSD_CONV_DIM
    dt_hi = dt_lo + 2 * SSD_HEADS
    pad = jnp.zeros((w.shape[0], DT_PAD - 2 * SSD_HEADS), w.dtype)
    return jnp.concatenate([w[:, :dt_lo], w[:, dt_hi:], w[:, dt_lo:dt_hi], pad], axis=1).astype(BF16)


def kernel(x, mem, ln_in_g, ln_in_b, w_in, na_rpb, ssd_conv_w, ssd_conv_b, ssd_dt_bias, ssd_a_log, ssd_d,
           ssd_norm_w, s5_lam_re, s5_lam_im, s5_log_dt, s5_b_re, s5_b_im, s5_c_re, s5_c_im, s5_d, s5_glu_w,
           s5_glu_b, w_mix_out, ln_mix_g, ln_mix_b, xa_wq, xa_wk, xa_wv, xa_wo, ln_xa_g, ln_xa_b, mlp_w1,
           mlp_w2, ln_mlp_g, ln_mlp_b):
    bsz, t, _ = x.shape
    n = bsz * t
    rows = t // GRID_W
    assert n % ROW_TILE == 0 and t % (SSD_BLOCK_CHUNKS * SSD_CHUNK) == 0 and rows % NA_Q_ROWS == 0
    assert rows >= NA_K_ROWS and t % S5_CHUNK == 0
    row1 = lambda v: v.reshape(1, -1).astype(F32)
    mem2 = mem.reshape(-1, D_MODEL)
    h = x.reshape(n, D_MODEL)
    for l in range(DEPTH):
        outs = _inproj(h, row1(ln_in_g), row1(ln_in_b), _permute_w_in(w_in[l]), apply_ln=(l == 0))
        qkv, z, xbc, u, dt_raw = outs[:5]
        if l == 0:
            h = outs[5]
        o_na = _na(qkv, _na_bias_tables(na_rpb[l].astype(F32), rows), bsz, t)

        xbc3 = xbc.reshape(bsz, t, SSD_CONV_DIM)
        dt4 = dt_raw[:, :2 * SSD_HEADS].reshape(bsz, t, 2, SSD_HEADS).transpose(2, 0, 1, 3)
        dtt4 = dt4.transpose(0, 1, 3, 2)
        conv_w, conv_b = ssd_conv_w[l].astype(F32), row1(ssd_conv_b[l])
        dt_bias, a_log = ssd_dt_bias[l].astype(F32), ssd_a_log[l].astype(F32)
        prev_f = _ssd_states(xbc3, dt4, conv_w, conv_b, dt_bias.reshape(2, 1, SSD_HEADS),
                             a_log.reshape(2, 1, SSD_HEADS), backward=False)
        prev_b = _ssd_states(xbc3, dt4, conv_w, conv_b, dt_bias.reshape(2, 1, SSD_HEADS),
                             a_log.reshape(2, 1, SSD_HEADS), backward=True)
        o_ssd = _ssd_out(xbc3, z.reshape(bsz, t, SSD_DIM), dt4, dtt4, prev_f, prev_b, conv_w, conv_b,
                         dt_bias, a_log, row1(jnp.repeat(ssd_d[l], HEAD_DIM)), row1(ssd_norm_w[l]))

        tables = _s5_tables(s5_lam_re[l].astype(F32), s5_lam_im[l].astype(F32), s5_log_dt[l].astype(F32),
                            s5_b_re[l].astype(F32), s5_b_im[l].astype(F32), s5_c_re[l].astype(F32),
                            s5_c_im[l].astype(F32))
        y5 = _s5(u, tables, s5_d[l].astype(F32), bsz, t)

        h = _mixout(h, o_na, o_ssd, y5, s5_glu_w[l].astype(BF16), row1(s5_glu_b[l]),
                    w_mix_out[l].astype(BF16), row1(ln_mix_g[l]), row1(ln_mix_b[l]))
        k2, v2 = _kvproj(mem2, xa_wk[l].astype(BF16), xa_wv[l].astype(BF16))
        h = _xattn(h, k2.reshape(bsz, -1, D_MODEL), v2.reshape(bsz, -1, D_MODEL), xa_wq[l].astype(BF16),
                   xa_wo[l].astype(BF16), row1(ln_xa_g[l]), row1(ln_xa_b[l]), bsz, t)
        h = _mlp(h, mlp_w1[l].astype(BF16), mlp_w2[l].astype(BF16), row1(ln_mlp_g[l]), row1(ln_mlp_b[l]))
    return h.reshape(bsz, t, D_MODEL)
```

```python
import functools
import math

import jax
import jax.numpy as jnp
import numpy as np
from jax import lax
from jax.experimental import pallas as pl
from jax.experimental.pallas import tpu as pltpu

F32 = jnp.float32
BF16 = jnp.bfloat16
HIGHEST = lax.Precision.HIGHEST

D_MODEL = 1024
DEPTH = 2
GRID_W = 64
HEAD_DIM = 64
NA_DIM = 256
NA_HEADS = 4
NA_WIN_ROWS = 8
NA_WIN_COLS = 16
SSD_DIM = 512
SSD_HEADS = 8
SSD_GROUPS = 2
SSD_STATE = 64
SSD_CONV = 5
SSD_CHUNK = 128
SSD_CONV_DIM = SSD_DIM + 2 * SSD_GROUPS * SSD_STATE
S5_DIM = 256
S5_GROUP_CH = 16
S5_GROUPS = 16
S5_STATE = 64
XA_HEADS = 4
XA_HEAD_DIM = 256
D_FF = 4096
LN_EPS = 1e-5
NEG_BIG = -1e30
DEEPNORM_ALPHA = (2 * DEPTH) ** 0.25

ROW_TILE = 512
NA_Q_ROWS = 4
NA_K_ROWS = NA_Q_ROWS + NA_WIN_ROWS
SSD_BLOCK_CHUNKS = 4
S5_CHUNK = 16
HALO = 8
DT_PAD = 128
VMEM_LIMIT = 56 * 1024 * 1024


def _params(*sem):
    return pltpu.CompilerParams(dimension_semantics=sem, vmem_limit_bytes=VMEM_LIMIT)


def _layernorm(x, g, b):
    mu = jnp.mean(x, axis=-1, keepdims=True)
    xc = x - mu
    var = jnp.mean(xc * xc, axis=-1, keepdims=True)
    return xc * lax.rsqrt(var + LN_EPS) * g + b


def _silu(x):
    return x / (1.0 + jnp.exp(-x))


def _softplus(x):
    return jnp.maximum(x, 0.0) + jnp.log1p(jnp.exp(-jnp.abs(x)))


def _const_spec(shape):
    n = len(shape)
    return pl.BlockSpec(shape, lambda *_: (0,) * n)


IN_SPLITS = ((0, 768), (768, 1280), (1280, 2048), (2048, 2304), (2304, 2304 + DT_PAD))
IN_DTYPES = (BF16, F32, F32, F32, F32)


def _inproj_kernel(x_ref, g_ref, b_ref, w_ref, *out_refs, apply_ln):
    x = x_ref[...]
    if apply_ln:
        x = _layernorm(x, g_ref[...], b_ref[...])
        out_refs[5][...] = x
    xb = x.astype(BF16)
    for ref, (lo, hi) in zip(out_refs[:5], IN_SPLITS):
        ref[...] = jnp.dot(xb, w_ref[:, lo:hi], preferred_element_type=F32).astype(ref.dtype)


def _inproj(x, g, b, w, apply_ln):
    n = x.shape[0]
    widths = [hi - lo for lo, hi in IN_SPLITS]
    out_shape = [jax.ShapeDtypeStruct((n, wd), dt) for wd, dt in zip(widths, IN_DTYPES)]
    out_specs = [pl.BlockSpec((ROW_TILE, wd), lambda i: (i, 0)) for wd in widths]
    if apply_ln:
        out_shape.append(jax.ShapeDtypeStruct((n, D_MODEL), F32))
        out_specs.append(pl.BlockSpec((ROW_TILE, D_MODEL), lambda i: (i, 0)))
    return pl.pallas_call(
        functools.partial(_inproj_kernel, apply_ln=apply_ln),
        grid=(n // ROW_TILE,),
        in_specs=[pl.BlockSpec((ROW_TILE, D_MODEL), lambda i: (i, 0)),
                  _const_spec((1, D_MODEL)), _const_spec((1, D_MODEL)),
                  _const_spec(w.shape)],
        out_specs=out_specs, out_shape=out_shape,
        compiler_params=_params("parallel"), name="in_proj",
    )(x, g, b, w)


def _na_bias_tables(rpb, rows):
    n_ri, n_ci = 2 * NA_WIN_ROWS - 1, 2 * NA_WIN_COLS - 1
    qr = np.arange(NA_Q_ROWS)
    kr = np.arange(NA_K_ROWS)
    c = np.arange(GRID_W)
    c0 = np.clip(c - NA_WIN_COLS // 2, 0, GRID_W - NA_WIN_COLS)
    col_ok = (c[None, :] >= c0[:, None]) & (c[None, :] < c0[:, None] + NA_WIN_COLS)
    ci = np.clip(c[None, :] - c[:, None], -(NA_WIN_COLS - 1), NA_WIN_COLS - 1) + (NA_WIN_COLS - 1)
    onehot_ci = (ci[None] == np.arange(n_ci)[:, None, None]).astype(np.float32)
    onehot_ri, ok = [], []
    for blk_row in (0, NA_Q_ROWS, rows - NA_Q_ROWS):
        start = min(max(blk_row - NA_WIN_ROWS // 2, 0), rows - NA_K_ROWS)
        r = blk_row + qr
        r0 = np.clip(r - NA_WIN_ROWS // 2, 0, rows - NA_WIN_ROWS)
        key_row = start + kr
        row_ok = (key_row[None, :] >= r0[:, None]) & (key_row[None, :] < r0[:, None] + NA_WIN_ROWS)
        ri = key_row[None, :] - r[:, None] + (NA_WIN_ROWS - 1)
        onehot_ri.append(((ri[..., None] == np.arange(n_ri)) & row_ok[..., None]).astype(np.float32))
        ok.append(row_ok[:, None, :, None] & col_ok[None, :, None, :])
    col_tab = jnp.einsum('hrc,cqk->hrqk', rpb, jnp.asarray(onehot_ci), precision=HIGHEST)
    bias = jnp.einsum('aqjr,hrwk->ahqwjk', jnp.asarray(np.stack(onehot_ri)), col_tab, precision=HIGHEST)
    bias = jnp.where(jnp.asarray(np.stack(ok))[:, None], bias, NEG_BIG)
    return bias.reshape(3, NA_HEADS, NA_Q_ROWS * GRID_W, NA_K_ROWS * GRID_W).astype(F32)


def _na_kernel(q_ref, k_ref, v_ref, bias_ref, o_ref, *, rows):
    i = pl.program_id(1)
    nblk = rows // NA_Q_ROWS
    case = jnp.where(i == 0, 0, jnp.where(i == nblk - 1, 2, 1))
    start_row = jnp.clip(i * NA_Q_ROWS - NA_WIN_ROWS // 2, 0, rows - NA_K_ROWS)
    start = pl.multiple_of(start_row * GRID_W, GRID_W)
    nk = NA_K_ROWS * GRID_W
    q = q_ref[...] * (HEAD_DIM ** -0.5)
    kw = k_ref[pl.ds(start, nk), :]
    vw = v_ref[pl.ds(start, nk), :]
    for h in range(NA_HEADS):
        sl = slice(h * HEAD_DIM, (h + 1) * HEAD_DIM)
        s = lax.dot_general(q[:, sl], kw[:, sl], (((1,), (1,)), ((), ())), preferred_element_type=F32)
        s = s + bias_ref[case, h]
        m = jnp.max(s, axis=-1, keepdims=True)
        p = jnp.exp(s - m)
        l = jnp.sum(p, axis=-1, keepdims=True)
        o = jnp.dot(p.astype(BF16), vw[:, sl], preferred_element_type=F32)
        o_ref[:, sl] = (o / l).astype(o_ref.dtype)


def _na(qkv, bias, bsz, t):
    rows = t // GRID_W
    tq = NA_Q_ROWS * GRID_W
    qkv3 = qkv.reshape(bsz, t, 3 * NA_DIM)
    out = pl.pallas_call(
        functools.partial(_na_kernel, rows=rows),
        grid=(bsz, rows // NA_Q_ROWS),
        in_specs=[pl.BlockSpec((None, tq, NA_DIM), lambda b, i: (b, i, 0)),
                  pl.BlockSpec((None, t, NA_DIM), lambda b, i: (b, 0, 1)),
                  pl.BlockSpec((None, t, NA_DIM), lambda b, i: (b, 0, 2)),
                  pl.BlockSpec(bias.shape, lambda b, i: (0, 0, 0, 0), pipeline_mode=pl.Buffered(1))],
        out_specs=pl.BlockSpec((None, tq, NA_DIM), lambda b, i: (b, i, 0)),
        out_shape=jax.ShapeDtypeStruct((bsz, t, NA_DIM), BF16),
        compiler_params=_params("parallel", "arbitrary"), name="na_attn",
    )(qkv3, qkv3, qkv3, bias)
    return out.reshape(bsz * t, NA_DIM)


def _conv_silu(xm_ref, xp_ref, xn_ref, cw_ref, cb_ref, ext_ref, blk, nblk, ncols):
    tb = xm_ref.shape[0]
    pad = SSD_CONV // 2
    ext_ref[0:HALO, :] = jnp.where(blk == 0, 0.0, xp_ref[...])
    ext_ref[HALO:HALO + tb, :] = xm_ref[...]
    ext_ref[HALO + tb:HALO + tb + HALO, :] = jnp.where(blk == nblk - 1, 0.0, xn_ref[...])
    acc = cb_ref[:, :ncols]
    for k in range(SSD_CONV):
        acc = acc + ext_ref[HALO - pad + k:HALO - pad + k + tb, :ncols] * cw_ref[k:k + 1, :ncols]
    return _silu(acc)


def _tri(n, lower):
    r = lax.broadcasted_iota(jnp.int32, (n, n), 0)
    c = lax.broadcasted_iota(jnp.int32, (n, n), 1)
    return (r >= c) if lower else (r <= c)


def _head_expand():
    r = lax.broadcasted_iota(jnp.int32, (SSD_HEADS, SSD_DIM), 0)
    c = lax.broadcasted_iota(jnp.int32, (SSD_HEADS, SSD_DIM), 1)
    return (c // HEAD_DIM == r).astype(F32)


def _ssd_state_kernel(xm_ref, xp_ref, xn_ref, dt_ref, cw_ref, cb_ref, dtb_ref, alog_ref,
                      prev_ref, ext_ref, state_ref, *, backward, nblk):
    j = pl.program_id(1)
    blk = (nblk - 1 - j) if backward else j
    q = SSD_CHUNK

    @pl.when(j == 0)
    def _():
        state_ref[...] = jnp.zeros_like(state_ref)

    ncols = SSD_DIM + SSD_GROUPS * SSD_STATE
    xc = _conv_silu(xm_ref, xp_ref, xn_ref, cw_ref, cb_ref, ext_ref, blk, nblk, ncols)
    a = -jnp.exp(alog_ref[...])
    dt = _softplus(dt_ref[...] + dtb_ref[...])
    tri = _tri(q, lower=not backward).astype(F32)
    expand = _head_expand()
    order = range(SSD_BLOCK_CHUNKS - 1, -1, -1) if backward else range(SSD_BLOCK_CHUNKS)
    for cc in order:
        rs = slice(cc * q, (cc + 1) * q)
        dtc = dt[rs]
        da = dtc * a
        cs = jnp.dot(tri, da, precision=HIGHEST, preferred_element_type=F32)
        tot = jnp.sum(da, axis=0, keepdims=True)
        w = jnp.exp(tot - cs) * dtc
        xw = (xc[rs, :SSD_DIM] * jnp.dot(w, expand, precision=HIGHEST, preferred_element_type=F32)).astype(BF16)
        chunk_decay = jnp.dot(jnp.broadcast_to(jnp.exp(tot), (8, SSD_HEADS)), expand,
                              precision=HIGHEST, preferred_element_type=F32)[0:1]
        prev_ref[cc] = state_ref[...]
        for g in range(SSD_GROUPS):
            bg = xc[rs, SSD_DIM + g * SSD_STATE:SSD_DIM + (g + 1) * SSD_STATE].astype(BF16)
            cols = slice(g * 256, (g + 1) * 256)
            s_g = lax.dot_general(bg, xw[:, cols], (((0,), (0,)), ((), ())), preferred_element_type=F32)
            state_ref[:, cols] = state_ref[:, cols] * chunk_decay[:, cols] + s_g


def _ssd_states(xbc3, dt4, conv_w, conv_b, dt_bias, a_log, backward):
    bsz, t, _ = xbc3.shape
    tb = SSD_BLOCK_CHUNKS * SSD_CHUNK
    nblk = t // tb
    hb = tb // HALO
    d = 1 if backward else 0
    eff = (lambda j: nblk - 1 - j) if backward else (lambda j: j)
    return pl.pallas_call(
        functools.partial(_ssd_state_kernel, backward=backward, nblk=nblk),
        grid=(bsz, nblk),
        in_specs=[pl.BlockSpec((None, tb, SSD_CONV_DIM), lambda b, j: (b, eff(j), 0)),
                  pl.BlockSpec((None, HALO, SSD_CONV_DIM), lambda b, j: (b, jnp.maximum(eff(j) * hb - 1, 0), 0)),
                  pl.BlockSpec((None, HALO, SSD_CONV_DIM),
                               lambda b, j: (b, jnp.minimum((eff(j) + 1) * hb, t // HALO - 1), 0)),
                  pl.BlockSpec((None, None, tb, SSD_HEADS), lambda b, j: (d, b, eff(j), 0)),
                  _const_spec((SSD_CONV, SSD_CONV_DIM)), _const_spec((1, SSD_CONV_DIM)),
                  pl.BlockSpec((None, 1, SSD_HEADS), lambda b, j: (d, 0, 0)),
                  pl.BlockSpec((None, 1, SSD_HEADS), lambda b, j: (d, 0, 0))],
        out_specs=pl.BlockSpec((None, SSD_BLOCK_CHUNKS, SSD_STATE, SSD_DIM), lambda b, j: (b, eff(j), 0, 0)),
        out_shape=jax.ShapeDtypeStruct((bsz, t // SSD_CHUNK, SSD_STATE, SSD_DIM), F32),
        scratch_shapes=[pltpu.VMEM((tb + 2 * HALO, SSD_CONV_DIM), F32),
                        pltpu.VMEM((SSD_STATE, SSD_DIM), F32)],
        compiler_params=_params("parallel", "arbitrary"),
        name="ssd_state_bwd" if backward else "ssd_state_fwd",
    )(xbc3, xbc3, xbc3, dt4, conv_w, conv_b, dt_bias, a_log)


def _ssd_out_kernel(xm_ref, xp_ref, xn_ref, z_ref, dtf_ref, dtb_ref, dtft_ref, dtbt_ref, pf_ref, pb_ref,
                    cw_ref, cb_ref, bias_row_ref, bias_col_ref, alog_row_ref, alog_col_ref,
                    dskip_ref, nw_ref, o_ref, ext_ref, y_ref, *, nblk):
    blk = pl.program_id(1)
    q = SSD_CHUNK
    xc = _conv_silu(xm_ref, xp_ref, xn_ref, cw_ref, cb_ref, ext_ref, blk, nblk, SSD_CONV_DIM)
    expand = _head_expand()
    lower = _tri(q, True)
    upper = _tri(q, False)
    a_row = -jnp.exp(alog_row_ref[...])
    a_col = -jnp.exp(alog_col_ref[...])
    dt_col = (_softplus(dtf_ref[...] + bias_row_ref[0]), _softplus(dtb_ref[...] + bias_row_ref[1]))
    dt_row = (_softplus(dtft_ref[...] + bias_col_ref[0]), _softplus(dtbt_ref[...] + bias_col_ref[1]))
    prev = (pf_ref, pb_ref)
    masks = (lower, upper)
    for cc in range(SSD_BLOCK_CHUNKS):
        rs = slice(cc * q, (cc + 1) * q)
        xs = xc[rs, :SSD_DIM]
        xsb = xs.astype(BF16)
        y = dskip_ref[...] * xs
        cs_col, cs_row, dtr = [], [], []
        for d in range(2):
            tri = masks[d].astype(F32)
            da_col = dt_col[d][rs] * a_row[d]
            dtr_d = dt_row[d][:, rs]
            da_row = dtr_d * a_col[d]
            cs_col.append(jnp.dot(tri, da_col, precision=HIGHEST, preferred_element_type=F32))
            cs_row.append(lax.dot_general(da_row, tri, (((1,), (1,)), ((), ())), precision=HIGHEST,
                                          preferred_element_type=F32))
            dtr.append(dtr_d)
        for g in range(SSD_GROUPS):
            off = SSD_DIM + g * SSD_STATE
            bg = xc[rs, off:off + SSD_STATE].astype(BF16)
            cg = xc[rs, off + SSD_GROUPS * SSD_STATE:off + SSD_GROUPS * SSD_STATE + SSD_STATE].astype(BF16)
            cb = lax.dot_general(cg, bg, (((1,), (1,)), ((), ())), preferred_element_type=F32)
            cols = slice(g * 256, (g + 1) * 256)
            y_off = None
            for d in range(2):
                e = jnp.dot(jnp.exp(cs_col[d]), expand[:, cols], precision=HIGHEST, preferred_element_type=F32)
                t_d = jnp.dot(cg, prev[d][cc, :, cols].astype(BF16), preferred_element_type=F32) * e
                y_off = t_d if y_off is None else y_off + t_d
            y_ref[:, cols] = y[:, cols] + y_off
            for hh in range(SSD_HEADS // SSD_GROUPS):
                h = g * (SSD_HEADS // SSD_GROUPS) + hh
                lmat = None
                for d in range(2):
                    seg = cs_col[d][:, h:h + 1] - cs_row[d][h:h + 1, :]
                    l_d = jnp.exp(jnp.where(masks[d], seg, NEG_BIG)) * dtr[d][h:h + 1, :]
                    lmat = l_d if lmat is None else lmat + l_d
                hs = slice(h * HEAD_DIM, (h + 1) * HEAD_DIM)
                y_ref[:, hs] += jnp.dot((cb * lmat).astype(BF16), xsb[:, hs], preferred_element_type=F32)
        yg = y_ref[...] * _silu(z_ref[rs, :])
        ms = jnp.mean(yg * yg, axis=-1, keepdims=True)
        o_ref[rs, :] = (yg * lax.rsqrt(ms + LN_EPS) * nw_ref[...]).astype(o_ref.dtype)


def _ssd_out(xbc3, z3, dt4, dtt4, prev_f, prev_b, conv_w, conv_b, dt_bias, a_log, d_skip, norm_w):
    bsz, t, _ = xbc3.shape
    tb = SSD_BLOCK_CHUNKS * SSD_CHUNK
    nblk = t // tb
    hb = tb // HALO
    out = pl.pallas_call(
        functools.partial(_ssd_out_kernel, nblk=nblk),
        grid=(bsz, nblk),
        in_specs=[pl.BlockSpec((None, tb, SSD_CONV_DIM), lambda b, j: (b, j, 0)),
                  pl.BlockSpec((None, HALO, SSD_CONV_DIM), lambda b, j: (b, jnp.maximum(j * hb - 1, 0), 0)),
                  pl.BlockSpec((None, HALO, SSD_CONV_DIM),
                               lambda b, j: (b, jnp.minimum((j + 1) * hb, t // HALO - 1), 0)),
                  pl.BlockSpec((None, tb, SSD_DIM), lambda b, j: (b, j, 0)),
                  pl.BlockSpec((None, None, tb, SSD_HEADS), lambda b, j: (0, b, j, 0)),
                  pl.BlockSpec((None, None, tb, SSD_HEADS), lambda b, j: (1, b, j, 0)),
                  pl.BlockSpec((None, None, SSD_HEADS, tb), lambda b, j: (0, b, 0, j)),
                  pl.BlockSpec((None, None, SSD_HEADS, tb), lambda b, j: (1, b, 0, j)),
                  pl.BlockSpec((None, SSD_BLOCK_CHUNKS, SSD_STATE, SSD_DIM), lambda b, j: (b, j, 0, 0)),
                  pl.BlockSpec((None, SSD_BLOCK_CHUNKS, SSD_STATE, SSD_DIM), lambda b, j: (b, j, 0, 0)),
                  _const_spec((SSD_CONV, SSD_CONV_DIM)), _const_spec((1, SSD_CONV_DIM)),
                  _const_spec((2, 1, SSD_HEADS)), _const_spec((2, SSD_HEADS, 1)),
                  _const_spec((2, 1, SSD_HEADS)), _const_spec((2, SSD_HEADS, 1)),
                  _const_spec((1, SSD_DIM)), _const_spec((1, SSD_DIM))],
        out_specs=pl.BlockSpec((None, tb, SSD_DIM), lambda b, j: (b, j, 0)),
        out_shape=jax.ShapeDtypeStruct((bsz, t, SSD_DIM), BF16),
        scratch_shapes=[pltpu.VMEM((tb + 2 * HALO, SSD_CONV_DIM), F32),
                        pltpu.VMEM((SSD_CHUNK, SSD_DIM), F32)],
        compiler_params=_params("parallel", "arbitrary"), name="ssd_out",
    )(xbc3, xbc3, xbc3, z3, dt4, dt4, dtt4, dtt4, prev_f, prev_b, conv_w, conv_b,
      dt_bias.reshape(2, 1, SSD_HEADS), dt_bias.reshape(2, SSD_HEADS, 1),
      a_log.reshape(2, 1, SSD_HEADS), a_log.reshape(2, SSD_HEADS, 1), d_skip, norm_w)
    return out.reshape(bsz * t, SSD_DIM)


def _s5_tables(lam_re, lam_im, log_dt, b_re, b_im, c_re, c_im):
    lc = S5_CHUNK
    hp = HIGHEST
    pw_re, pw_im, bb_re, bb_im = [], [], [], []
    for d in range(2):
        dt = jnp.exp(log_dt[d])[:, None]
        lr, li = lam_re[d], lam_im[d]
        mag = jnp.exp(lr * dt)
        ar, ai = mag * jnp.cos(li * dt), mag * jnp.sin(li * dt)
        den = lr * lr + li * li
        fr = ((ar - 1.0) * lr + ai * li) / den
        fi = (ai * lr - (ar - 1.0) * li) / den
        bb_re.append(fr[..., None] * b_re[d] - fi[..., None] * b_im[d])
        bb_im.append(fr[..., None] * b_im[d] + fi[..., None] * b_re[d])
        pr, pi = [jnp.ones_like(ar)], [jnp.zeros_like(ar)]
        for _ in range(lc):
            pr, pi = pr + [pr[-1] * ar - pi[-1] * ai], pi + [pr[-1] * ai + pi[-1] * ar]
        pw_re.append(jnp.stack(pr))
        pw_im.append(jnp.stack(pi))

    def kern(d):
        wr = pw_re[d][:lc, :, :, None] * bb_re[d][None] - pw_im[d][:lc, :, :, None] * bb_im[d][None]
        wi = pw_re[d][:lc, :, :, None] * bb_im[d][None] + pw_im[d][:lc, :, :, None] * bb_re[d][None]
        return (jnp.einsum('ghp,kgpj->kghj', c_re[d], wr, precision=hp)
                - jnp.einsum('ghp,kgpj->kghj', c_im[d], wi, precision=hp))

    kf, kb = kern(0), kern(1)
    l = jnp.arange(lc)
    lag = l[:, None] - l[None, :]
    tf = jnp.where((lag >= 0)[:, :, None, None, None], kf[jnp.clip(lag, 0, lc - 1)], 0.0)
    tb = jnp.where((lag <= 0)[:, :, None, None, None], kb[jnp.clip(-lag, 0, lc - 1)], 0.0)
    toep = (tf + tb).transpose(2, 1, 4, 0, 3).reshape(S5_GROUPS, lc * S5_GROUP_CH, lc * S5_GROUP_CH)

    def state_in(d, powers):
        wr = pw_re[d][powers][:, :, :, None] * bb_re[d][None] - pw_im[d][powers][:, :, :, None] * bb_im[d][None]
        wi = pw_re[d][powers][:, :, :, None] * bb_im[d][None] + pw_im[d][powers][:, :, :, None] * bb_re[d][None]
        to_rows = lambda w: w.transpose(1, 0, 3, 2).reshape(S5_GROUPS, lc * S5_GROUP_CH, S5_STATE)
        return to_rows(wr), to_rows(wi)

    f_re, f_im = state_in(0, lc - 1 - l)
    b_re_, b_im_ = state_in(1, l)
    m_state = jnp.concatenate([f_re, b_re_, f_im, b_im_], axis=-1)

    def state_out(d, powers):
        cpr = c_re[d][None] * pw_re[d][powers][:, :, None, :] - c_im[d][None] * pw_im[d][powers][:, :, None, :]
        cpi = c_re[d][None] * pw_im[d][powers][:, :, None, :] + c_im[d][None] * pw_re[d][powers][:, :, None, :]
        to_cols = lambda w: w.transpose(1, 3, 0, 2).reshape(S5_GROUPS, S5_STATE, lc * S5_GROUP_CH)
        return to_cols(cpr), -to_cols(cpi)

    of_re, of_im = state_out(0, l + 1)
    ob_re, ob_im = state_out(1, lc - l)
    m_off = jnp.concatenate([of_re, ob_re, of_im, ob_im], axis=1)
    dec_re = jnp.concatenate([pw_re[0][lc], pw_re[1][lc]], axis=-1)[:, None, :]
    dec_im = jnp.concatenate([pw_im[0][lc], pw_im[1][lc]], axis=-1)[:, None, :]
    return toep.astype(BF16), m_state.astype(BF16), m_off.astype(BF16), dec_re, dec_im


S5_W = S5_CHUNK * S5_GROUP_CH
S5_RELAYOUT_CHUNKS = 64


def _s5_group_kernel(u_ref, o_ref):
    x = u_ref[...]
    for g in range(S5_GROUPS):
        lo = g * S5_GROUP_CH
        pieces = [x[:, s * S5_DIM + lo:s * S5_DIM + lo + S5_GROUP_CH] for s in range(S5_CHUNK)]
        o_ref[g] = jnp.concatenate(pieces, axis=-1).astype(o_ref.dtype)


def _s5_ungroup_kernel(y_ref, o_ref):
    ys = [y_ref[g] for g in range(S5_GROUPS)]
    for l in range(S5_CHUNK):
        pieces = [y[:, l * S5_GROUP_CH:(l + 1) * S5_GROUP_CH] for y in ys]
        o_ref[:, l * S5_DIM:(l + 1) * S5_DIM] = jnp.concatenate(pieces, axis=-1)


def _s5_kernel(u_ref, toep_ref, mst_ref, moff_ref, are_ref, aim_ref, y_ref, s_ref, e_ref, *, nc, bsz):
    ub = u_ref[...]
    y_ref[...] = jnp.dot(ub, toep_ref[...], preferred_element_type=F32)
    s_ref[...] = jnp.dot(ub, mst_ref[...], preferred_element_type=F32)
    half = S5_STATE
    ar = jnp.broadcast_to(are_ref[...], (bsz, 2 * half))
    ai = jnp.broadcast_to(aim_ref[...], (bsz, 2 * half))
    is_fwd = lax.broadcasted_iota(jnp.int32, (bsz, 2 * half), 1) < half

    def body(i, carry):
        er, ei = carry
        rf = pl.multiple_of(i * bsz, bsz)
        rb = pl.multiple_of((nc - 1 - i) * bsz, bsz)
        e_ref[pl.ds(rf, bsz), 0:half] = er[:, :half]
        e_ref[pl.ds(rb, bsz), half:2 * half] = er[:, half:]
        e_ref[pl.ds(rf, bsz), 2 * half:3 * half] = ei[:, :half]
        e_ref[pl.ds(rb, bsz), 3 * half:4 * half] = ei[:, half:]
        sf = s_ref[pl.ds(rf, bsz), :]
        sb = s_ref[pl.ds(rb, bsz), :]
        sr = jnp.where(is_fwd, sf[:, :2 * half], sb[:, :2 * half])
        si = jnp.where(is_fwd, sf[:, 2 * half:], sb[:, 2 * half:])
        return ar * er - ai * ei + sr, ar * ei + ai * er + si

    zero = jnp.zeros((bsz, 2 * half), F32)
    lax.fori_loop(0, nc, body, (zero, zero))
    y_ref[...] += jnp.dot(e_ref[...].astype(BF16), moff_ref[...], preferred_element_type=F32)


def _s5(u, tables, bsz, t):
    toep, m_state, m_off, dec_re, dec_im = tables
    nc = t // S5_CHUNK
    w = S5_W
    rc = min(S5_RELAYOUT_CHUNKS, nc)
    tok = pl.BlockSpec((None, rc, S5_CHUNK * S5_DIM), lambda b, j: (b, j, 0))
    grouped = pl.BlockSpec((S5_GROUPS, rc, w), lambda b, j: (0, j, b))
    ug = pl.pallas_call(
        _s5_group_kernel, grid=(bsz, nc // rc), in_specs=[tok], out_specs=grouped,
        out_shape=jax.ShapeDtypeStruct((S5_GROUPS, nc, bsz * w), BF16),
        compiler_params=_params("parallel", "parallel"), name="s5_group",
    )(u.reshape(bsz, nc, S5_CHUNK * S5_DIM))
    grp = lambda shape: pl.BlockSpec((None,) + shape, lambda g: (g, 0, 0))
    yg = pl.pallas_call(
        functools.partial(_s5_kernel, nc=nc, bsz=bsz),
        grid=(S5_GROUPS,),
        in_specs=[grp((nc * bsz, w)), grp((w, w)), grp((w, w)), grp((w, w)),
                  grp((1, 2 * S5_STATE)), grp((1, 2 * S5_STATE))],
        out_specs=grp((nc * bsz, w)),
        out_shape=jax.ShapeDtypeStruct((S5_GROUPS, nc * bsz, w), F32),
        scratch_shapes=[pltpu.VMEM((nc * bsz, w), F32), pltpu.VMEM((nc * bsz, w), F32)],
        compiler_params=_params("parallel"), name="s5_scan",
    )(ug.reshape(S5_GROUPS, nc * bsz, w), toep, m_state, m_off, dec_re, dec_im)
    y = pl.pallas_call(
        _s5_ungroup_kernel, grid=(bsz, nc // rc), in_specs=[grouped], out_specs=tok,
        out_shape=jax.ShapeDtypeStruct((bsz, nc, S5_CHUNK * S5_DIM), F32),
        compiler_params=_params("parallel", "parallel"), name="s5_ungroup",
    )(yg.reshape(S5_GROUPS, nc, bsz * w))
    return y.reshape(bsz * t, S5_DIM)


def _mixout_kernel(h_ref, na_ref, ssd_ref, y5_ref, u_ref, d5_ref, gw_ref, gb_ref, wo_ref, g_ref, b_ref, o_ref):
    y5 = y5_ref[...] + d5_ref[...] * u_ref[...]
    gl = 0.5 * y5 * (1.0 + jnp.tanh(math.sqrt(2.0 / math.pi) * (y5 + 0.044715 * (y5 * y5 * y5))))
    gate = jnp.dot(gl.astype(BF16), gw_ref[...], preferred_element_type=F32) + gb_ref[...]
    o5 = gl / (1.0 + jnp.exp(-gate))
    mix = jnp.dot(na_ref[...], wo_ref[0:NA_DIM, :], preferred_element_type=F32)
    mix += jnp.dot(ssd_ref[...], wo_ref[NA_DIM:NA_DIM + SSD_DIM, :], preferred_element_type=F32)
    mix += jnp.dot(o5.astype(BF16), wo_ref[NA_DIM + SSD_DIM:, :], preferred_element_type=F32)
    o_ref[...] = _layernorm(DEEPNORM_ALPHA * h_ref[...] + mix, g_ref[...], b_ref[...])


def _mixout(h, o_na, o_ssd, y5, u, d5, glu_w, glu_b, w_out, g, b):
    n = h.shape[0]
    row = lambda wd: pl.BlockSpec((ROW_TILE, wd), lambda i: (i, 0))
    return pl.pallas_call(
        _mixout_kernel, grid=(n // ROW_TILE,),
        in_specs=[row(D_MODEL), row(NA_DIM), row(SSD_DIM), row(S5_DIM), row(S5_DIM), _const_spec((1, S5_DIM)),
                  _const_spec((S5_DIM, S5_DIM)), _const_spec((1, S5_DIM)),
                  _const_spec((D_MODEL, D_MODEL)), _const_spec((1, D_MODEL)), _const_spec((1, D_MODEL))],
        out_specs=row(D_MODEL), out_shape=jax.ShapeDtypeStruct((n, D_MODEL), F32),
        compiler_params=_params("parallel"), name="mix_out",
    )(h, o_na, o_ssd, y5, u, d5, glu_w, glu_b, w_out, g, b)


def _kvproj_kernel(m_ref, wk_ref, wv_ref, k_ref, v_ref):
    mb = m_ref[...].astype(BF16)
    k_ref[...] = jnp.dot(mb, wk_ref[...], preferred_element_type=F32).astype(k_ref.dtype)
    v_ref[...] = jnp.dot(mb, wv_ref[...], preferred_element_type=F32).astype(v_ref.dtype)


def _kvproj(mem2, wk, wv):
    n = mem2.shape[0]
    tm = min(ROW_TILE, n)
    row = pl.BlockSpec((tm, D_MODEL), lambda i: (i, 0))
    return pl.pallas_call(
        _kvproj_kernel, grid=(n // tm,),
        in_specs=[row, _const_spec((D_MODEL, D_MODEL)), _const_spec((D_MODEL, D_MODEL))],
        out_specs=[row, row], out_shape=[jax.ShapeDtypeStruct((n, D_MODEL), BF16)] * 2,
        compiler_params=_params("parallel"), name="xa_kv_proj",
    )(mem2, wk, wv)


def _xattn_kernel(h_ref, k_ref, v_ref, wq_ref, wo_ref, g_ref, b_ref, o_ref):
    h = h_ref[...]
    q = jnp.dot(h.astype(BF16), wq_ref[...], preferred_element_type=F32) * (XA_HEAD_DIM ** -0.5)
    qb = q.astype(BF16)
    xa = None
    for hd in range(XA_HEADS):
        sl = slice(hd * XA_HEAD_DIM, (hd + 1) * XA_HEAD_DIM)
        s = lax.dot_general(qb[:, sl], k_ref[:, sl], (((1,), (1,)), ((), ())), preferred_element_type=F32)
        m = jnp.max(s, axis=-1, keepdims=True)
        p = jnp.exp(s - m)
        l = jnp.sum(p, axis=-1, keepdims=True)
        o = jnp.dot(p.astype(BF16), v_ref[:, sl], preferred_element_type=F32) / l
        part = jnp.dot(o.astype(BF16), wo_ref[sl, :], preferred_element_type=F32)
        xa = part if xa is None else xa + part
    o_ref[...] = _layernorm(DEEPNORM_ALPHA * h + xa, g_ref[...], b_ref[...])


def _xattn(h, k3, v3, wq, wo, g, b, bsz, t):
    m = k3.shape[1]
    h3 = h.reshape(bsz, t, D_MODEL)
    row = pl.BlockSpec((None, ROW_TILE, D_MODEL), lambda bb, i: (bb, i, 0))
    kv = pl.BlockSpec((None, m, D_MODEL), lambda bb, i: (bb, 0, 0))
    const = lambda shape: pl.BlockSpec(shape, lambda bb, i: (0, 0))
    out = pl.pallas_call(
        _xattn_kernel, grid=(bsz, t // ROW_TILE),
        in_specs=[row, kv, kv, const((D_MODEL, D_MODEL)), const((D_MODEL, D_MODEL)),
                  const((1, D_MODEL)), const((1, D_MODEL))],
        out_specs=row, out_shape=jax.ShapeDtypeStruct((bsz, t, D_MODEL), F32),
        compiler_params=_params("parallel", "parallel"), name="cross_attn",
    )(h3, k3, v3, wq, wo, g, b)
    return out.reshape(bsz * t, D_MODEL)


FF_TILE = 1024


def _mlp_kernel(h_ref, w1_ref, w2_ref, g_ref, b_ref, o_ref):
    h = h_ref[...]
    hb = h.astype(BF16)
    ff = None
    for c in range(D_FF // FF_TILE):
        sl = slice(c * FF_TILE, (c + 1) * FF_TILE)
        a = jnp.maximum(jnp.dot(hb, w1_ref[:, sl], preferred_element_type=F32), 0.0)
        part = jnp.dot((a * a).astype(BF16), w2_ref[sl, :], preferred_element_type=F32)
        ff = part if ff is None else ff + part
    o_ref[...] = _layernorm(DEEPNORM_ALPHA * h + ff, g_ref[...], b_ref[...])


def _mlp(h, w1, w2, g, b):
    n = h.shape[0]
    row = pl.BlockSpec((ROW_TILE, D_MODEL), lambda i: (i, 0))
    return pl.pallas_call(
        _mlp_kernel, grid=(n // ROW_TILE,),
        in_specs=[row,
                  pl.BlockSpec((D_MODEL, D_FF), lambda i: (0, 0), pipeline_mode=pl.Buffered(1)),
                  pl.BlockSpec((D_FF, D_MODEL), lambda i: (0, 0), pipeline_mode=pl.Buffered(1)),
                  _const_spec((1, D_MODEL)), _const_spec((1, D_MODEL))],
        out_specs=row, out_shape=jax.ShapeDtypeStruct((n, D_MODEL), F32),
        compiler_params=_params("parallel"), name="mlp",
    )(h, w1, w2, g, b)


def _permute_w_in(w):
    dt_lo = 3 * NA_DIM + SSD_DIM + SSD_CONV_DIM
    dt_hi = dt_lo + 2 * SSD_HEADS
    pad = jnp.zeros((w.shape[0], DT_PAD - 2 * SSD_HEADS), w.dtype)
    return jnp.concatenate([w[:, :dt_lo], w[:, dt_hi:], w[:, dt_lo:dt_hi], pad], axis=1).astype(BF16)


def kernel(x, mem, ln_in_g, ln_in_b, w_in, na_rpb, ssd_conv_w, ssd_conv_b, ssd_dt_bias, ssd_a_log, ssd_d,
           ssd_norm_w, s5_lam_re, s5_lam_im, s5_log_dt, s5_b_re, s5_b_im, s5_c_re, s5_c_im, s5_d, s5_glu_w,
           s5_glu_b, w_mix_out, ln_mix_g, ln_mix_b, xa_wq, xa_wk, xa_wv, xa_wo, ln_xa_g, ln_xa_b, mlp_w1,
           mlp_w2, ln_mlp_g, ln_mlp_b):
    bsz, t, _ = x.shape
    n = bsz * t
    rows = t // GRID_W
    assert n % ROW_TILE == 0 and t % (SSD_BLOCK_CHUNKS * SSD_CHUNK) == 0 and rows % NA_Q_ROWS == 0
    assert rows >= NA_K_ROWS and t % S5_CHUNK == 0
    row1 = lambda v: v.reshape(1, -1).astype(F32)
    mem2 = mem.reshape(-1, D_MODEL)
    h = x.reshape(n, D_MODEL)
    for l in range(DEPTH):
        outs = _inproj(h, row1(ln_in_g), row1(ln_in_b), _permute_w_in(w_in[l]), apply_ln=(l == 0))
        qkv, z, xbc, u, dt_raw = outs[:5]
        if l == 0:
            h = outs[5]
        o_na = _na(qkv, _na_bias_tables(na_rpb[l].astype(F32), rows), bsz, t)

        xbc3 = xbc.reshape(bsz, t, SSD_CONV_DIM)
        dt4 = dt_raw[:, :2 * SSD_HEADS].reshape(bsz, t, 2, SSD_HEADS).transpose(2, 0, 1, 3)
        dtt4 = dt4.transpose(0, 1, 3, 2)
        conv_w, conv_b = ssd_conv_w[l].astype(F32), row1(ssd_conv_b[l])
        dt_bias, a_log = ssd_dt_bias[l].astype(F32), ssd_a_log[l].astype(F32)
        prev_f = _ssd_states(xbc3, dt4, conv_w, conv_b, dt_bias.reshape(2, 1, SSD_HEADS),
                             a_log.reshape(2, 1, SSD_HEADS), backward=False)
        prev_b = _ssd_states(xbc3, dt4, conv_w, conv_b, dt_bias.reshape(2, 1, SSD_HEADS),
                             a_log.reshape(2, 1, SSD_HEADS), backward=True)
        o_ssd = _ssd_out(xbc3, z.reshape(bsz, t, SSD_DIM), dt4, dtt4, prev_f, prev_b, conv_w, conv_b,
                         dt_bias, a_log, row1(jnp.repeat(ssd_d[l], HEAD_DIM)), row1(ssd_norm_w[l]))

        tables = _s5_tables(s5_lam_re[l].astype(F32), s5_lam_im[l].astype(F32), s5_log_dt[l].astype(F32),
                            s5_b_re[l].astype(F32), s5_b_im[l].astype(F32), s5_c_re[l].astype(F32),
                            s5_c_im[l].astype(F32))
        y5 = _s5(u, tables, bsz, t)

        h = _mixout(h, o_na, o_ssd, y5, u, row1(s5_d[l]), s5_glu_w[l].astype(BF16), row1(s5_glu_b[l]),
                    w_mix_out[l].astype(BF16), row1(ln_mix_g[l]), row1(ln_mix_b[l]))
        k2, v2 = _kvproj(mem2, xa_wk[l].astype(BF16), xa_wv[l].astype(BF16))
        h = _xattn(h, k2.reshape(bsz, -1, D_MODEL), v2.reshape(bsz, -1, D_MODEL), xa_wq[l].astype(BF16),
                   xa_wo[l].astype(BF16), row1(ln_xa_g[l]), row1(ln_xa_b[l]), bsz, t)
        h = _mlp(h, mlp_w1[l].astype(BF16), mlp_w2[l].astype(BF16), row1(ln_mlp_g[l]), row1(ln_mlp_b[l]))
    return h.reshape(bsz, t, D_MODEL)
```

```python
import functools
import math

import jax
import jax.numpy as jnp
import numpy as np
from jax import lax
from jax.experimental import pallas as pl
from jax.experimental.pallas import tpu as pltpu

F32 = jnp.float32
BF16 = jnp.bfloat16
HIGHEST = lax.Precision.HIGHEST

D_MODEL = 1024
DEPTH = 2
GRID_W = 64
HEAD_DIM = 64
NA_DIM = 256
NA_HEADS = 4
NA_WIN_ROWS = 8
NA_WIN_COLS = 16
SSD_DIM = 512
SSD_HEADS = 8
SSD_GROUPS = 2
SSD_STATE = 64
SSD_CONV = 5
SSD_CHUNK = 128
SSD_CONV_DIM = SSD_DIM + 2 * SSD_GROUPS * SSD_STATE
S5_DIM = 256
S5_GROUP_CH = 16
S5_GROUPS = 16
S5_STATE = 64
XA_HEADS = 4
XA_HEAD_DIM = 256
D_FF = 4096
LN_EPS = 1e-5
NEG_BIG = -1e30
DEEPNORM_ALPHA = (2 * DEPTH) ** 0.25

ROW_TILE = 512
NA_Q_ROWS = 4
NA_K_ROWS = NA_Q_ROWS + NA_WIN_ROWS
SSD_BLOCK_CHUNKS = 4
S5_CHUNK = 16
DT_PAD = 128
VMEM_LIMIT = 56 * 1024 * 1024


def _params(*sem):
    return pltpu.CompilerParams(dimension_semantics=sem, vmem_limit_bytes=VMEM_LIMIT)


def _layernorm(x, g, b):
    mu = jnp.mean(x, axis=-1, keepdims=True)
    xc = x - mu
    var = jnp.mean(xc * xc, axis=-1, keepdims=True)
    return xc * lax.rsqrt(var + LN_EPS) * g + b


def _sigmoid(x):
    return 0.5 + 0.5 * jnp.tanh(0.5 * x)


def _silu(x):
    return x * _sigmoid(x)


def _softplus(x):
    return jnp.maximum(x, 0.0) + jnp.log1p(jnp.exp(-jnp.abs(x)))


def _const_spec(shape):
    n = len(shape)
    return pl.BlockSpec(shape, lambda *_: (0,) * n)


IN_QKV, IN_Z, IN_XBC, IN_U, IN_DT = (0, 768), (768, 1280), (1280, 2048), (2048, 2304), (2304, 2304 + DT_PAD)
PROJ_HALO = 16


def _inproj_kernel(x_ref, xp_ref, xn_ref, g_ref, b_ref, w_ref, cw_ref, cb_ref, dtb_ref,
                   qkv_ref, z_ref, xc_ref, u_ref, dt_ref, dtt_ref, *rest, apply_ln, tiles_per_seq):
    xe_ref, ce_ref = rest[-2:]
    tm, hl = ROW_TILE, PROJ_HALO
    pos = pl.program_id(0) % tiles_per_seq
    x, xp, xn = x_ref[...], xp_ref[...], xn_ref[...]
    if apply_ln:
        x = _layernorm(x, g_ref[...], b_ref[...])
        xp = _layernorm(xp, g_ref[...], b_ref[...])
        xn = _layernorm(xn, g_ref[...], b_ref[...])
        rest[0][...] = x
    xb = x.astype(BF16)
    xe_ref[0:hl, :] = xp.astype(BF16)
    xe_ref[hl:hl + tm, :] = xb
    xe_ref[hl + tm:, :] = xn.astype(BF16)

    def proj(cols):
        return jnp.dot(xb, w_ref[:, cols[0]:cols[1]], preferred_element_type=F32)

    qkv_ref[...] = proj(IN_QKV).astype(qkv_ref.dtype)
    z_ref[...] = proj(IN_Z)
    u_ref[...] = proj(IN_U)
    dt = _softplus(proj(IN_DT) + dtb_ref[...])
    dt_ref[...] = dt[:, :2 * SSD_HEADS]
    dtt_ref[...] = dt.T[:2 * SSD_HEADS, :]

    ce_ref[...] = jnp.dot(xe_ref[...], w_ref[:, IN_XBC[0]:IN_XBC[1]], preferred_element_type=F32)

    @pl.when(pos == 0)
    def _():
        ce_ref[0:hl, :] = jnp.zeros((hl, SSD_CONV_DIM), F32)

    @pl.when(pos == tiles_per_seq - 1)
    def _():
        ce_ref[hl + tm:, :] = jnp.zeros((hl, SSD_CONV_DIM), F32)

    pad = SSD_CONV // 2
    acc = cb_ref[...]
    for k in range(SSD_CONV):
        acc = acc + ce_ref[hl - pad + k:hl - pad + k + tm, :] * cw_ref[k:k + 1, :]
    xc_ref[...] = _silu(acc).astype(xc_ref.dtype)


def _inproj(x, g, b, w, conv_w, conv_b, dt_bias, t, apply_ln):
    n = x.shape[0]
    tm, hl = ROW_TILE, PROJ_HALO
    per = tm // hl
    row = lambda wd: pl.BlockSpec((tm, wd), lambda i: (i, 0))
    out_shape = [jax.ShapeDtypeStruct((n, 3 * NA_DIM), BF16), jax.ShapeDtypeStruct((n, SSD_DIM), F32),
                 jax.ShapeDtypeStruct((n, SSD_CONV_DIM), BF16), jax.ShapeDtypeStruct((n, S5_DIM), F32),
                 jax.ShapeDtypeStruct((n, 2 * SSD_HEADS), F32), jax.ShapeDtypeStruct((2 * SSD_HEADS, n), F32)]
    out_specs = [row(3 * NA_DIM), row(SSD_DIM), row(SSD_CONV_DIM), row(S5_DIM), row(2 * SSD_HEADS),
                 pl.BlockSpec((2 * SSD_HEADS, tm), lambda i: (0, i))]
    if apply_ln:
        out_shape.append(jax.ShapeDtypeStruct((n, D_MODEL), F32))
        out_specs.append(row(D_MODEL))
    return pl.pallas_call(
        functools.partial(_inproj_kernel, apply_ln=apply_ln, tiles_per_seq=t // tm),
        grid=(n // tm,),
        in_specs=[row(D_MODEL),
                  pl.BlockSpec((hl, D_MODEL), lambda i: (jnp.maximum(i * per - 1, 0), 0)),
                  pl.BlockSpec((hl, D_MODEL), lambda i: (jnp.minimum((i + 1) * per, n // hl - 1), 0)),
                  _const_spec((1, D_MODEL)), _const_spec((1, D_MODEL)), _const_spec(w.shape),
                  _const_spec((SSD_CONV, SSD_CONV_DIM)), _const_spec((1, SSD_CONV_DIM)), _const_spec((1, DT_PAD))],
        out_specs=out_specs, out_shape=out_shape,
        scratch_shapes=[pltpu.VMEM((tm + 2 * hl, D_MODEL), BF16), pltpu.VMEM((tm + 2 * hl, SSD_CONV_DIM), F32)],
        compiler_params=_params("parallel"), name="in_proj",
    )(x, x, x, g, b, w, conv_w, conv_b, dt_bias)


def _na_bias_tables(rpb, rows):
    n_ri, n_ci = 2 * NA_WIN_ROWS - 1, 2 * NA_WIN_COLS - 1
    qr = np.arange(NA_Q_ROWS)
    kr = np.arange(NA_K_ROWS)
    c = np.arange(GRID_W)
    c0 = np.clip(c - NA_WIN_COLS // 2, 0, GRID_W - NA_WIN_COLS)
    col_ok = (c[None, :] >= c0[:, None]) & (c[None, :] < c0[:, None] + NA_WIN_COLS)
    ci = np.clip(c[None, :] - c[:, None], -(NA_WIN_COLS - 1), NA_WIN_COLS - 1) + (NA_WIN_COLS - 1)
    onehot_ci = (ci[None] == np.arange(n_ci)[:, None, None]).astype(np.float32)
    onehot_ri, ok = [], []
    for blk_row in (0, NA_Q_ROWS, rows - NA_Q_ROWS):
        start = min(max(blk_row - NA_WIN_ROWS // 2, 0), rows - NA_K_ROWS)
        r = blk_row + qr
        r0 = np.clip(r - NA_WIN_ROWS // 2, 0, rows - NA_WIN_ROWS)
        key_row = start + kr
        row_ok = (key_row[None, :] >= r0[:, None]) & (key_row[None, :] < r0[:, None] + NA_WIN_ROWS)
        ri = key_row[None, :] - r[:, None] + (NA_WIN_ROWS - 1)
        onehot_ri.append(((ri[..., None] == np.arange(n_ri)) & row_ok[..., None]).astype(np.float32))
        ok.append(row_ok[:, None, :, None] & col_ok[None, :, None, :])
    col_tab = jnp.einsum('hrc,cqk->hrqk', rpb, jnp.asarray(onehot_ci), precision=HIGHEST)
    bias = jnp.einsum('aqjr,hrwk->ahqwjk', jnp.asarray(np.stack(onehot_ri)), col_tab, precision=HIGHEST)
    bias = jnp.where(jnp.asarray(np.stack(ok))[:, None], bias, NEG_BIG)
    return bias.reshape(3, NA_HEADS, NA_Q_ROWS * GRID_W, NA_K_ROWS * GRID_W).astype(F32)


def _na_kernel(q_ref, k_ref, v_ref, bias_ref, o_ref, *, rows):
    i = pl.program_id(1)
    nblk = rows // NA_Q_ROWS
    case = jnp.where(i == 0, 0, jnp.where(i == nblk - 1, 2, 1))
    start_row = jnp.clip(i * NA_Q_ROWS - NA_WIN_ROWS // 2, 0, rows - NA_K_ROWS)
    start = pl.multiple_of(start_row * GRID_W, GRID_W)
    nk = NA_K_ROWS * GRID_W
    q = q_ref[...] * (HEAD_DIM ** -0.5)
    kw = k_ref[pl.ds(start, nk), :]
    vw = v_ref[pl.ds(start, nk), :]
    for h in range(NA_HEADS):
        sl = slice(h * HEAD_DIM, (h + 1) * HEAD_DIM)
        s = lax.dot_general(q[:, sl], kw[:, sl], (((1,), (1,)), ((), ())), preferred_element_type=F32)
        s = s + bias_ref[case, h]
        m = jnp.max(s, axis=-1, keepdims=True)
        p = jnp.exp(s - m)
        l = jnp.sum(p, axis=-1, keepdims=True)
        o = jnp.dot(p.astype(BF16), vw[:, sl], preferred_element_type=F32)
        o_ref[:, sl] = (o / l).astype(o_ref.dtype)


def _na(qkv, bias, bsz, t):
    rows = t // GRID_W
    tq = NA_Q_ROWS * GRID_W
    qkv3 = qkv.reshape(bsz, t, 3 * NA_DIM)
    out = pl.pallas_call(
        functools.partial(_na_kernel, rows=rows),
        grid=(bsz, rows // NA_Q_ROWS),
        in_specs=[pl.BlockSpec((None, tq, NA_DIM), lambda b, i: (b, i, 0)),
                  pl.BlockSpec((None, t, NA_DIM), lambda b, i: (b, 0, 1)),
                  pl.BlockSpec((None, t, NA_DIM), lambda b, i: (b, 0, 2)),
                  pl.BlockSpec(bias.shape, lambda b, i: (0, 0, 0, 0), pipeline_mode=pl.Buffered(1))],
        out_specs=pl.BlockSpec((None, tq, NA_DIM), lambda b, i: (b, i, 0)),
        out_shape=jax.ShapeDtypeStruct((bsz, t, NA_DIM), BF16),
        compiler_params=_params("parallel", "arbitrary"), name="na_attn",
    )(qkv3, qkv3, qkv3, bias)
    return out.reshape(bsz * t, NA_DIM)


def _tri(n, lower):
    r = lax.broadcasted_iota(jnp.int32, (n, n), 0)
    c = lax.broadcasted_iota(jnp.int32, (n, n), 1)
    return (r >= c) if lower else (r <= c)


def _split3(x):
    hi = x.astype(BF16).astype(F32)
    r = x - hi
    mid = r.astype(BF16).astype(F32)
    lo = (r - mid).astype(BF16).astype(F32)
    return hi, mid, lo


def _dot01_left(m01, x):
    return sum(jnp.dot(m01, p.astype(BF16), preferred_element_type=F32) for p in _split3(x))


def _dot01_right(x, m01):
    return sum(jnp.dot(p.astype(BF16), m01, preferred_element_type=F32) for p in _split3(x))


def _head_expand():
    r = lax.broadcasted_iota(jnp.int32, (SSD_HEADS, SSD_DIM), 0)
    c = lax.broadcasted_iota(jnp.int32, (SSD_HEADS, SSD_DIM), 1)
    return (c // HEAD_DIM == r).astype(BF16)


def _chunk_masks():
    tb = SSD_BLOCK_CHUNKS * SSD_CHUNK
    r = np.arange(tb)
    same = (r[:, None] // SSD_CHUNK) == (r[None, :] // SSD_CHUNK)
    lower = same & (r[:, None] >= r[None, :])
    upper = same & (r[:, None] <= r[None, :])
    return tuple(jnp.asarray(m, dtype=BF16) for m in (lower, upper, same))


def _ssd_state_kernel(xf_ref, xb_ref, dtf_ref, dtb_ref, alog_ref, tril_ref, triu_ref, ones_ref,
                      pf_ref, pb_ref, sf_ref, sb_ref):
    q = SSD_CHUNK

    @pl.when(pl.program_id(1) == 0)
    def _():
        sf_ref[...] = jnp.zeros_like(sf_ref)
        sb_ref[...] = jnp.zeros_like(sb_ref)

    a_all = -jnp.exp(alog_ref[...])
    expand = _head_expand()
    for d, (x_ref, dt_ref, tri_ref, prev_ref, state_ref) in enumerate(
            ((xf_ref, dtf_ref, tril_ref, pf_ref, sf_ref), (xb_ref, dtb_ref, triu_ref, pb_ref, sb_ref))):
        hs = slice(d * SSD_HEADS, (d + 1) * SSD_HEADS)
        dt = dt_ref[...][:, hs]
        da = dt * a_all[:, hs]
        cs = _dot01_left(tri_ref[...], da)
        tot = _dot01_left(ones_ref[...], da)
        w = jnp.exp(tot - cs) * dt
        xw = (x_ref[:, :SSD_DIM].astype(F32) * _dot01_right(w, expand)).astype(BF16)
        decay = _dot01_right(jnp.exp(tot), expand)
        bm = x_ref[:, SSD_DIM:SSD_DIM + SSD_GROUPS * SSD_STATE]
        state = [state_ref[:, g * 256:(g + 1) * 256] for g in range(SSD_GROUPS)]
        order = range(SSD_BLOCK_CHUNKS - 1, -1, -1) if d == 1 else range(SSD_BLOCK_CHUNKS)
        for cc in order:
            rs = slice(cc * q, (cc + 1) * q)
            for g in range(SSD_GROUPS):
                cols = slice(g * 256, (g + 1) * 256)
                prev_ref[cc, :, cols] = state[g].astype(prev_ref.dtype)
                s_g = lax.dot_general(bm[rs, g * SSD_STATE:(g + 1) * SSD_STATE], xw[rs, cols],
                                      (((0,), (0,)), ((), ())), preferred_element_type=F32)
                state[g] = state[g] * decay[cc * q:cc * q + 1, cols] + s_g
        for g in range(SSD_GROUPS):
            state_ref[:, g * 256:(g + 1) * 256] = state[g]


def _ssd_states(xc3, dt3, a_log16):
    bsz, t, _ = xc3.shape
    tb = SSD_BLOCK_CHUNKS * SSD_CHUNK
    nblk = t // tb
    fwd = lambda wd: pl.BlockSpec((None, tb, wd), lambda b, j: (b, j, 0))
    bwd = lambda wd: pl.BlockSpec((None, tb, wd), lambda b, j: (b, nblk - 1 - j, 0))
    state = jax.ShapeDtypeStruct((bsz, t // SSD_CHUNK, SSD_STATE, SSD_DIM), BF16)
    return pl.pallas_call(
        _ssd_state_kernel, grid=(bsz, nblk),
        in_specs=[fwd(SSD_CONV_DIM), bwd(SSD_CONV_DIM), fwd(2 * SSD_HEADS), bwd(2 * SSD_HEADS),
                  _const_spec((1, 2 * SSD_HEADS))] + [_const_spec((tb, tb))] * 3,
        out_specs=[pl.BlockSpec((None, SSD_BLOCK_CHUNKS, SSD_STATE, SSD_DIM), lambda b, j: (b, j, 0, 0)),
                   pl.BlockSpec((None, SSD_BLOCK_CHUNKS, SSD_STATE, SSD_DIM),
                                lambda b, j: (b, nblk - 1 - j, 0, 0))],
        out_shape=[state, state],
        scratch_shapes=[pltpu.VMEM((SSD_STATE, SSD_DIM), F32), pltpu.VMEM((SSD_STATE, SSD_DIM), F32)],
        compiler_params=_params("parallel", "arbitrary"), name="ssd_state",
    )(xc3, xc3, dt3, dt3, a_log16, *_chunk_masks())


def _ssd_out_kernel(x_ref, z_ref, dt_ref, dtt_ref, pf_ref, pb_ref, alog_row_ref, alog_col_ref,
                    dskip_ref, nw_ref, o_ref):
    q = SSD_CHUNK
    nh = SSD_HEADS
    row_id = lax.broadcasted_iota(jnp.int32, (q, q), 0)
    col_id = lax.broadcasted_iota(jnp.int32, (q, q), 1)
    lower = row_id >= col_id
    eye = row_id == col_id
    first_half = lax.broadcasted_iota(jnp.int32, (q, 2 * HEAD_DIM), 1) < HEAD_DIM
    tri_l = lower.astype(BF16)
    tri_u = (row_id <= col_id).astype(BF16)
    a_row = -jnp.exp(alog_row_ref[...])
    a_col = -jnp.exp(alog_col_ref[...])
    dt_all = dt_ref[...]
    dtt_all = dtt_ref[...]
    for cc in range(SSD_BLOCK_CHUNKS):
        rs = slice(cc * q, (cc + 1) * q)
        dtr = dtt_all[:, rs]
        da_col = dt_all[rs] * a_row
        da_row = dtr * a_col
        cs_col_f, cs_col_b = _dot01_left(tri_l, da_col), _dot01_left(tri_u, da_col)
        cs_row_f, cs_row_b = _dot01_right(da_row, tri_u), _dot01_right(da_row, tri_l)
        e_col_f, e_col_b = jnp.exp(cs_col_f), jnp.exp(cs_col_b)
        bc = x_ref[rs, SSD_DIM:SSD_CONV_DIM]
        groups = []
        for g in range(SSD_GROUPS):
            bg = bc[:, g * SSD_STATE:(g + 1) * SSD_STATE]
            cg = bc[:, (SSD_GROUPS + g) * SSD_STATE:(SSD_GROUPS + g + 1) * SSD_STATE]
            cb = lax.dot_general(cg, bg, (((1,), (1,)), ((), ())), preferred_element_type=F32)
            c2 = jnp.concatenate([cg, cg], axis=1).astype(F32)
            pairs = []
            for pp in range(SSD_HEADS // SSD_GROUPS // 2):
                h0 = g * (SSD_HEADS // SSD_GROUPS) + 2 * pp
                lanes = slice(h0 * HEAD_DIM, (h0 + 2) * HEAD_DIM)
                rhs = jnp.concatenate([x_ref[rs, lanes], pf_ref[cc, :, lanes], pb_ref[cc, :, lanes]], axis=0)
                outs = []
                for h in (h0, h0 + 1):
                    seg_f = cs_col_f[:, h:h + 1] - cs_row_f[h:h + 1, :]
                    seg_b = cs_col_b[:, nh + h:nh + h + 1] - cs_row_b[nh + h:nh + h + 1, :]
                    dt_f, dt_b = dtr[h:h + 1, :], dtr[nh + h:nh + h + 1, :]
                    lmat = (jnp.exp(jnp.where(lower, seg_f, seg_b)) * jnp.where(lower, dt_f, dt_b)
                            + jnp.where(eye, dt_b, 0.0))
                    e2 = jnp.where(first_half, e_col_f[:, h:h + 1], e_col_b[:, nh + h:nh + h + 1])
                    lhs = jnp.concatenate([(cb * lmat).astype(BF16), (c2 * e2).astype(BF16)], axis=1)
                    outs.append(jnp.dot(lhs, rhs, preferred_element_type=F32))
                pairs.append(jnp.where(first_half, outs[0], outs[1]))
            groups.append(jnp.concatenate(pairs, axis=1))
        y = jnp.concatenate(groups, axis=1) + dskip_ref[...] * x_ref[rs, :SSD_DIM].astype(F32)
        yg = y * _silu(z_ref[rs, :])
        ms = jnp.mean(yg * yg, axis=-1, keepdims=True)
        o_ref[rs, :] = (yg * lax.rsqrt(ms + LN_EPS) * nw_ref[...]).astype(o_ref.dtype)


def _ssd_out(xc3, z3, dt3, dtt, prev_f, prev_b, a_log16, d_skip, norm_w):
    bsz, t, _ = xc3.shape
    tb = SSD_BLOCK_CHUNKS * SSD_CHUNK
    nblk = t // tb
    blk = lambda wd: pl.BlockSpec((None, tb, wd), lambda b, j: (b, j, 0))
    prev = pl.BlockSpec((None, SSD_BLOCK_CHUNKS, SSD_STATE, SSD_DIM), lambda b, j: (b, j, 0, 0))
    out = pl.pallas_call(
        _ssd_out_kernel, grid=(bsz, nblk),
        in_specs=[blk(SSD_CONV_DIM), blk(SSD_DIM), blk(2 * SSD_HEADS),
                  pl.BlockSpec((2 * SSD_HEADS, tb), lambda b, j: (0, b * nblk + j)),
                  prev, prev, _const_spec((1, 2 * SSD_HEADS)), _const_spec((2 * SSD_HEADS, 1)),
                  _const_spec((1, SSD_DIM)), _const_spec((1, SSD_DIM))],
        out_specs=blk(SSD_DIM),
        out_shape=jax.ShapeDtypeStruct((bsz, t, SSD_DIM), BF16),
        compiler_params=_params("parallel", "parallel"), name="ssd_out",
    )(xc3, z3, dt3, dtt, prev_f, prev_b, a_log16, a_log16.reshape(2 * SSD_HEADS, 1), d_skip, norm_w)
    return out.reshape(bsz * t, SSD_DIM)


def _s5_tables(lam_re, lam_im, log_dt, b_re, b_im, c_re, c_im):
    lc = S5_CHUNK
    hp = HIGHEST
    pw_re, pw_im, bb_re, bb_im = [], [], [], []
    for d in range(2):
        dt = jnp.exp(log_dt[d])[:, None]
        lr, li = lam_re[d], lam_im[d]
        mag = jnp.exp(lr * dt)
        ar, ai = mag * jnp.cos(li * dt), mag * jnp.sin(li * dt)
        den = lr * lr + li * li
        fr = ((ar - 1.0) * lr + ai * li) / den
        fi = (ai * lr - (ar - 1.0) * li) / den
        bb_re.append(fr[..., None] * b_re[d] - fi[..., None] * b_im[d])
        bb_im.append(fr[..., None] * b_im[d] + fi[..., None] * b_re[d])
        pr, pi = [jnp.ones_like(ar)], [jnp.zeros_like(ar)]
        for _ in range(lc):
            pr, pi = pr + [pr[-1] * ar - pi[-1] * ai], pi + [pr[-1] * ai + pi[-1] * ar]
        pw_re.append(jnp.stack(pr))
        pw_im.append(jnp.stack(pi))

    def kern(d):
        wr = pw_re[d][:lc, :, :, None] * bb_re[d][None] - pw_im[d][:lc, :, :, None] * bb_im[d][None]
        wi = pw_re[d][:lc, :, :, None] * bb_im[d][None] + pw_im[d][:lc, :, :, None] * bb_re[d][None]
        return (jnp.einsum('ghp,kgpj->kghj', c_re[d], wr, precision=hp)
                - jnp.einsum('ghp,kgpj->kghj', c_im[d], wi, precision=hp))

    kf, kb = kern(0), kern(1)
    l = jnp.arange(lc)
    lag = l[:, None] - l[None, :]
    tf = jnp.where((lag >= 0)[:, :, None, None, None], kf[jnp.clip(lag, 0, lc - 1)], 0.0)
    tb = jnp.where((lag <= 0)[:, :, None, None, None], kb[jnp.clip(-lag, 0, lc - 1)], 0.0)
    toep = (tf + tb).transpose(2, 1, 4, 0, 3).reshape(S5_GROUPS, lc * S5_GROUP_CH, lc * S5_GROUP_CH)

    def state_in(d, powers):
        wr = pw_re[d][powers][:, :, :, None] * bb_re[d][None] - pw_im[d][powers][:, :, :, None] * bb_im[d][None]
        wi = pw_re[d][powers][:, :, :, None] * bb_im[d][None] + pw_im[d][powers][:, :, :, None] * bb_re[d][None]
        to_rows = lambda w: w.transpose(1, 0, 3, 2).reshape(S5_GROUPS, lc * S5_GROUP_CH, S5_STATE)
        return to_rows(wr), to_rows(wi)

    f_re, f_im = state_in(0, lc - 1 - l)
    b_re_, b_im_ = state_in(1, l)
    m_state = jnp.concatenate([f_re, b_re_, f_im, b_im_], axis=-1)

    def state_out(d, powers):
        cpr = c_re[d][None] * pw_re[d][powers][:, :, None, :] - c_im[d][None] * pw_im[d][powers][:, :, None, :]
        cpi = c_re[d][None] * pw_im[d][powers][:, :, None, :] + c_im[d][None] * pw_re[d][powers][:, :, None, :]
        to_cols = lambda w: w.transpose(1, 3, 0, 2).reshape(S5_GROUPS, S5_STATE, lc * S5_GROUP_CH)
        return to_cols(cpr), -to_cols(cpi)

    of_re, of_im = state_out(0, l + 1)
    ob_re, ob_im = state_out(1, lc - l)
    m_off = jnp.concatenate([of_re, ob_re, of_im, ob_im], axis=1)
    dec_re = jnp.concatenate([pw_re[0][lc], pw_re[1][lc]], axis=-1)[:, None, :]
    dec_im = jnp.concatenate([pw_im[0][lc], pw_im[1][lc]], axis=-1)[:, None, :]
    return toep.astype(BF16), m_state.astype(BF16), m_off.astype(BF16), dec_re, dec_im


S5_W = S5_CHUNK * S5_GROUP_CH
S5_RELAYOUT_CHUNKS = 64


def _s5_group_kernel(u_ref, o_ref):
    x = u_ref[...]
    for g in range(S5_GROUPS):
        lo = g * S5_GROUP_CH
        pieces = [x[:, s * S5_DIM + lo:s * S5_DIM + lo + S5_GROUP_CH] for s in range(S5_CHUNK)]
        o_ref[g] = jnp.concatenate(pieces, axis=-1).astype(o_ref.dtype)


def _s5_ungroup_kernel(y_ref, o_ref):
    ys = [y_ref[g] for g in range(S5_GROUPS)]
    for l in range(S5_CHUNK):
        pieces = [y[:, l * S5_GROUP_CH:(l + 1) * S5_GROUP_CH] for y in ys]
        o_ref[:, l * S5_DIM:(l + 1) * S5_DIM] = jnp.concatenate(pieces, axis=-1)


def _s5_kernel(u_ref, toep_ref, mst_ref, moff_ref, are_ref, aim_ref, y_ref, s_ref, e_ref, *, nc, bsz):
    ub = u_ref[...]
    y_ref[...] = jnp.dot(ub, toep_ref[...], preferred_element_type=F32)
    s_ref[...] = jnp.dot(ub, mst_ref[...], preferred_element_type=F32)
    half = S5_STATE
    ar = jnp.broadcast_to(are_ref[...], (bsz, 2 * half))
    ai = jnp.broadcast_to(aim_ref[...], (bsz, 2 * half))
    is_fwd = lax.broadcasted_iota(jnp.int32, (bsz, 2 * half), 1) < half

    def body(i, carry):
        er, ei = carry
        rf = pl.multiple_of(i * bsz, bsz)
        rb = pl.multiple_of((nc - 1 - i) * bsz, bsz)
        e_ref[pl.ds(rf, bsz), 0:half] = er[:, :half]
        e_ref[pl.ds(rb, bsz), half:2 * half] = er[:, half:]
        e_ref[pl.ds(rf, bsz), 2 * half:3 * half] = ei[:, :half]
        e_ref[pl.ds(rb, bsz), 3 * half:4 * half] = ei[:, half:]
        sf = s_ref[pl.ds(rf, bsz), :]
        sb = s_ref[pl.ds(rb, bsz), :]
        sr = jnp.where(is_fwd, sf[:, :2 * half], sb[:, :2 * half])
        si = jnp.where(is_fwd, sf[:, 2 * half:], sb[:, 2 * half:])
        return ar * er - ai * ei + sr, ar * ei + ai * er + si

    zero = jnp.zeros((bsz, 2 * half), F32)
    lax.fori_loop(0, nc, body, (zero, zero))
    y_ref[...] += jnp.dot(e_ref[...].astype(BF16), moff_ref[...], preferred_element_type=F32)


def _s5(u, tables, bsz, t):
    toep, m_state, m_off, dec_re, dec_im = tables
    nc = t // S5_CHUNK
    w = S5_W
    rc = min(S5_RELAYOUT_CHUNKS, nc)
    tok = pl.BlockSpec((None, rc, S5_CHUNK * S5_DIM), lambda b, j: (b, j, 0))
    grouped = pl.BlockSpec((S5_GROUPS, rc, w), lambda b, j: (0, j, b))
    ug = pl.pallas_call(
        _s5_group_kernel, grid=(bsz, nc // rc), in_specs=[tok], out_specs=grouped,
        out_shape=jax.ShapeDtypeStruct((S5_GROUPS, nc, bsz * w), BF16),
        compiler_params=_params("parallel", "parallel"), name="s5_group",
    )(u.reshape(bsz, nc, S5_CHUNK * S5_DIM))
    grp = lambda shape: pl.BlockSpec((None,) + shape, lambda g: (g, 0, 0))
    yg = pl.pallas_call(
        functools.partial(_s5_kernel, nc=nc, bsz=bsz),
        grid=(S5_GROUPS,),
        in_specs=[grp((nc * bsz, w)), grp((w, w)), grp((w, w)), grp((w, w)),
                  grp((1, 2 * S5_STATE)), grp((1, 2 * S5_STATE))],
        out_specs=grp((nc * bsz, w)),
        out_shape=jax.ShapeDtypeStruct((S5_GROUPS, nc * bsz, w), F32),
        scratch_shapes=[pltpu.VMEM((nc * bsz, w), F32), pltpu.VMEM((nc * bsz, w), F32)],
        compiler_params=_params("parallel"), name="s5_scan",
    )(ug.reshape(S5_GROUPS, nc * bsz, w), toep, m_state, m_off, dec_re, dec_im)
    y = pl.pallas_call(
        _s5_ungroup_kernel, grid=(bsz, nc // rc), in_specs=[grouped], out_specs=tok,
        out_shape=jax.ShapeDtypeStruct((bsz, nc, S5_CHUNK * S5_DIM), F32),
        compiler_params=_params("parallel", "parallel"), name="s5_ungroup",
    )(yg.reshape(S5_GROUPS, nc, bsz * w))
    return y.reshape(bsz * t, S5_DIM)


def _mixout_kernel(h_ref, na_ref, ssd_ref, y5_ref, u_ref, d5_ref, gw_ref, gb_ref, wo_ref, g_ref, b_ref, o_ref):
    y5 = y5_ref[...] + d5_ref[...] * u_ref[...]
    gl = 0.5 * y5 * (1.0 + jnp.tanh(math.sqrt(2.0 / math.pi) * (y5 + 0.044715 * (y5 * y5 * y5))))
    gate = jnp.dot(gl.astype(BF16), gw_ref[...], preferred_element_type=F32) + gb_ref[...]
    o5 = gl * _sigmoid(gate)
    mix = jnp.dot(na_ref[...], wo_ref[0:NA_DIM, :], preferred_element_type=F32)
    mix += jnp.dot(ssd_ref[...], wo_ref[NA_DIM:NA_DIM + SSD_DIM, :], preferred_element_type=F32)
    mix += jnp.dot(o5.astype(BF16), wo_ref[NA_DIM + SSD_DIM:, :], preferred_element_type=F32)
    o_ref[...] = _layernorm(DEEPNORM_ALPHA * h_ref[...] + mix, g_ref[...], b_ref[...])


def _mixout(h, o_na, o_ssd, y5, u, d5, glu_w, glu_b, w_out, g, b):
    n = h.shape[0]
    row = lambda wd: pl.BlockSpec((ROW_TILE, wd), lambda i: (i, 0))
    return pl.pallas_call(
        _mixout_kernel, grid=(n // ROW_TILE,),
        in_specs=[row(D_MODEL), row(NA_DIM), row(SSD_DIM), row(S5_DIM), row(S5_DIM), _const_spec((1, S5_DIM)),
                  _const_spec((S5_DIM, S5_DIM)), _const_spec((1, S5_DIM)),
                  _const_spec((D_MODEL, D_MODEL)), _const_spec((1, D_MODEL)), _const_spec((1, D_MODEL))],
        out_specs=row(D_MODEL), out_shape=jax.ShapeDtypeStruct((n, D_MODEL), F32),
        compiler_params=_params("parallel"), name="mix_out",
    )(h, o_na, o_ssd, y5, u, d5, glu_w, glu_b, w_out, g, b)


def _kvproj_kernel(m_ref, wk_ref, wv_ref, k_ref, v_ref):
    mb = m_ref[...].astype(BF16)
    k_ref[...] = jnp.dot(mb, wk_ref[...], preferred_element_type=F32).astype(k_ref.dtype)
    v_ref[...] = jnp.dot(mb, wv_ref[...], preferred_element_type=F32).astype(v_ref.dtype)


def _kvproj(mem2, wk, wv):
    n = mem2.shape[0]
    tm = min(ROW_TILE, n)
    row = pl.BlockSpec((tm, D_MODEL), lambda i: (i, 0))
    return pl.pallas_call(
        _kvproj_kernel, grid=(n // tm,),
        in_specs=[row, _const_spec((D_MODEL, D_MODEL)), _const_spec((D_MODEL, D_MODEL))],
        out_specs=[row, row], out_shape=[jax.ShapeDtypeStruct((n, D_MODEL), BF16)] * 2,
        compiler_params=_params("parallel"), name="xa_kv_proj",
    )(mem2, wk, wv)


def _xattn_kernel(h_ref, k_ref, v_ref, wq_ref, wo_ref, g_ref, b_ref, o_ref):
    h = h_ref[...]
    q = jnp.dot(h.astype(BF16), wq_ref[...], preferred_element_type=F32) * (XA_HEAD_DIM ** -0.5)
    qb = q.astype(BF16)
    xa = None
    for hd in range(XA_HEADS):
        sl = slice(hd * XA_HEAD_DIM, (hd + 1) * XA_HEAD_DIM)
        s = lax.dot_general(qb[:, sl], k_ref[:, sl], (((1,), (1,)), ((), ())), preferred_element_type=F32)
        m = jnp.max(s, axis=-1, keepdims=True)
        p = jnp.exp(s - m)
        l = jnp.sum(p, axis=-1, keepdims=True)
        o = jnp.dot(p.astype(BF16), v_ref[:, sl], preferred_element_type=F32) / l
        part = jnp.dot(o.astype(BF16), wo_ref[sl, :], preferred_element_type=F32)
        xa = part if xa is None else xa + part
    o_ref[...] = _layernorm(DEEPNORM_ALPHA * h + xa, g_ref[...], b_ref[...])


def _xattn(h, k3, v3, wq, wo, g, b, bsz, t):
    m = k3.shape[1]
    h3 = h.reshape(bsz, t, D_MODEL)
    row = pl.BlockSpec((None, ROW_TILE, D_MODEL), lambda bb, i: (bb, i, 0))
    kv = pl.BlockSpec((None, m, D_MODEL), lambda bb, i: (bb, 0, 0))
    const = lambda shape: pl.BlockSpec(shape, lambda bb, i: (0, 0))
    out = pl.pallas_call(
        _xattn_kernel, grid=(bsz, t // ROW_TILE),
        in_specs=[row, kv, kv, const((D_MODEL, D_MODEL)), const((D_MODEL, D_MODEL)),
                  const((1, D_MODEL)), const((1, D_MODEL))],
        out_specs=row, out_shape=jax.ShapeDtypeStruct((bsz, t, D_MODEL), F32),
        compiler_params=_params("parallel", "parallel"), name="cross_attn",
    )(h3, k3, v3, wq, wo, g, b)
    return out.reshape(bsz * t, D_MODEL)


FF_TILE = 1024


def _mlp_kernel(h_ref, w1_ref, w2_ref, g_ref, b_ref, o_ref):
    h = h_ref[...]
    hb = h.astype(BF16)
    ff = None
    for c in range(D_FF // FF_TILE):
        sl = slice(c * FF_TILE, (c + 1) * FF_TILE)
        a = jnp.maximum(jnp.dot(hb, w1_ref[:, sl], preferred_element_type=F32), 0.0)
        part = jnp.dot((a * a).astype(BF16), w2_ref[sl, :], preferred_element_type=F32)
        ff = part if ff is None else ff + part
    o_ref[...] = _layernorm(DEEPNORM_ALPHA * h + ff, g_ref[...], b_ref[...])


def _mlp(h, w1, w2, g, b):
    n = h.shape[0]
    row = pl.BlockSpec((ROW_TILE, D_MODEL), lambda i: (i, 0))
    return pl.pallas_call(
        _mlp_kernel, grid=(n // ROW_TILE,),
        in_specs=[row,
                  pl.BlockSpec((D_MODEL, D_FF), lambda i: (0, 0), pipeline_mode=pl.Buffered(1)),
                  pl.BlockSpec((D_FF, D_MODEL), lambda i: (0, 0), pipeline_mode=pl.Buffered(1)),
                  _const_spec((1, D_MODEL)), _const_spec((1, D_MODEL))],
        out_specs=row, out_shape=jax.ShapeDtypeStruct((n, D_MODEL), F32),
        compiler_params=_params("parallel"), name="mlp",
    )(h, w1, w2, g, b)


def _permute_w_in(w):
    dt_lo = 3 * NA_DIM + SSD_DIM + SSD_CONV_DIM
    dt_hi = dt_lo + 2 * SSD_HEADS
    pad = jnp.zeros((w.shape[0], DT_PAD - 2 * SSD_HEADS), w.dtype)
    return jnp.concatenate([w[:, :dt_lo], w[:, dt_hi:], w[:, dt_lo:dt_hi], pad], axis=1).astype(BF16)


def kernel(x, mem, ln_in_g, ln_in_b, w_in, na_rpb, ssd_conv_w, ssd_conv_b, ssd_dt_bias, ssd_a_log, ssd_d,
           ssd_norm_w, s5_lam_re, s5_lam_im, s5_log_dt, s5_b_re, s5_b_im, s5_c_re, s5_c_im, s5_d, s5_glu_w,
           s5_glu_b, w_mix_out, ln_mix_g, ln_mix_b, xa_wq, xa_wk, xa_wv, xa_wo, ln_xa_g, ln_xa_b, mlp_w1,
           mlp_w2, ln_mlp_g, ln_mlp_b):
    bsz, t, _ = x.shape
    n = bsz * t
    rows = t // GRID_W
    assert n % ROW_TILE == 0 and t % (SSD_BLOCK_CHUNKS * SSD_CHUNK) == 0 and rows % NA_Q_ROWS == 0
    assert rows >= NA_K_ROWS and t % S5_CHUNK == 0
    row1 = lambda v: v.reshape(1, -1).astype(F32)
    mem2 = mem.reshape(-1, D_MODEL)
    h = x.reshape(n, D_MODEL)
    for l in range(DEPTH):
        dt_bias = jnp.pad(row1(ssd_dt_bias[l]), ((0, 0), (0, DT_PAD - 2 * SSD_HEADS)))
        outs = _inproj(h, row1(ln_in_g), row1(ln_in_b), _permute_w_in(w_in[l]), ssd_conv_w[l].astype(F32),
                       row1(ssd_conv_b[l]), dt_bias, t, apply_ln=(l == 0))
        qkv, z, xc, u, dt, dtt = outs[:6]
        if l == 0:
            h = outs[6]
        o_na = _na(qkv, _na_bias_tables(na_rpb[l].astype(F32), rows), bsz, t)

        xc3 = xc.reshape(bsz, t, SSD_CONV_DIM)
        dt3 = dt.reshape(bsz, t, 2 * SSD_HEADS)
        a_log16 = row1(ssd_a_log[l])
        prev_f, prev_b = _ssd_states(xc3, dt3, a_log16)
        o_ssd = _ssd_out(xc3, z.reshape(bsz, t, SSD_DIM), dt3, dtt, prev_f, prev_b, a_log16,
                         row1(jnp.repeat(ssd_d[l], HEAD_DIM)), row1(ssd_norm_w[l]))

        tables = _s5_tables(s5_lam_re[l].astype(F32), s5_lam_im[l].astype(F32), s5_log_dt[l].astype(F32),
                            s5_b_re[l].astype(F32), s5_b_im[l].astype(F32), s5_c_re[l].astype(F32),
                            s5_c_im[l].astype(F32))
        y5 = _s5(u, tables, bsz, t)

        h = _mixout(h, o_na, o_ssd, y5, u, row1(s5_d[l]), s5_glu_w[l].astype(BF16), row1(s5_glu_b[l]),
                    w_mix_out[l].astype(BF16), row1(ln_mix_g[l]), row1(ln_mix_b[l]))
        k2, v2 = _kvproj(mem2, xa_wk[l].astype(BF16), xa_wv[l].astype(BF16))
        h = _xattn(h, k2.reshape(bsz, -1, D_MODEL), v2.reshape(bsz, -1, D_MODEL), xa_wq[l].astype(BF16),
                   xa_wo[l].astype(BF16), row1(ln_xa_g[l]), row1(ln_xa_b[l]), bsz, t)
        h = _mlp(h, mlp_w1[l].astype(BF16), mlp_w2[l].astype(BF16), row1(ln_mlp_g[l]), row1(ln_mlp_b[l]))
    return h.reshape(bsz, t, D_MODEL)
```

```python
import functools
import math

import jax
import jax.numpy as jnp
import numpy as np
from jax import lax
from jax.experimental import pallas as pl
from jax.experimental.pallas import tpu as pltpu

F32 = jnp.float32
BF16 = jnp.bfloat16
HIGHEST = lax.Precision.HIGHEST

D_MODEL = 1024
DEPTH = 2
GRID_W = 64
HEAD_DIM = 64
NA_DIM = 256
NA_HEADS = 4
NA_WIN_ROWS = 8
NA_WIN_COLS = 16
SSD_DIM = 512
SSD_HEADS = 8
SSD_GROUPS = 2
SSD_STATE = 64
SSD_CONV = 5
SSD_CHUNK = 128
SSD_CONV_DIM = SSD_DIM + 2 * SSD_GROUPS * SSD_STATE
S5_DIM = 256
S5_GROUP_CH = 16
S5_GROUPS = 16
S5_STATE = 64
XA_HEADS = 4
XA_HEAD_DIM = 256
D_FF = 4096
LN_EPS = 1e-5
NEG_BIG = -1e30
DEEPNORM_ALPHA = (2 * DEPTH) ** 0.25

ROW_TILE = 512
NA_Q_ROWS = 4
NA_K_ROWS = NA_Q_ROWS + NA_WIN_ROWS
SSD_BLOCK_CHUNKS = 4
S5_CHUNK = 16
LANES = 128
DT_PAD = LANES
VMEM_LIMIT = 56 * 1024 * 1024


def _params(*sem):
    return pltpu.CompilerParams(dimension_semantics=sem, vmem_limit_bytes=VMEM_LIMIT)


def _layernorm(x, g, b):
    mu = jnp.mean(x, axis=-1, keepdims=True)
    xc = x - mu
    var = jnp.mean(xc * xc, axis=-1, keepdims=True)
    return xc * lax.rsqrt(var + LN_EPS) * g + b


def _sigmoid(x):
    return 0.5 + 0.5 * jnp.tanh(0.5 * x)


def _silu(x):
    return x * _sigmoid(x)


def _softplus(x):
    return jnp.maximum(x, 0.0) + jnp.log1p(jnp.exp(-jnp.abs(x)))


def _const_spec(shape):
    n = len(shape)
    return pl.BlockSpec(shape, lambda *_: (0,) * n)


IN_QKV, IN_Z, IN_XBC, IN_U, IN_DT = (0, 768), (768, 1280), (1280, 2048), (2048, 2304), (2304, 2304 + DT_PAD)
PROJ_HALO = 16


def _inproj_kernel(x_ref, xp_ref, xn_ref, g_ref, b_ref, w_ref, cw_ref, cb_ref, dtb_ref,
                   qkv_ref, z_ref, xc_ref, bt_ref, u_ref, dtt_ref, *rest, apply_ln, tiles_per_seq):
    xe_ref, ce_ref, us_ref = rest[-3:]
    tm, hl = ROW_TILE, PROJ_HALO
    pos = pl.program_id(0) % tiles_per_seq
    x, xp, xn = x_ref[...], xp_ref[...], xn_ref[...]
    if apply_ln:
        x = _layernorm(x, g_ref[...], b_ref[...])
        xp = _layernorm(xp, g_ref[...], b_ref[...])
        xn = _layernorm(xn, g_ref[...], b_ref[...])
        rest[0][...] = x
    xb = x.astype(BF16)
    xe_ref[0:hl, :] = xp.astype(BF16)
    xe_ref[hl:hl + tm, :] = xb
    xe_ref[hl + tm:, :] = xn.astype(BF16)

    xbc = jnp.dot(xe_ref[...], w_ref[:, IN_XBC[0]:IN_XBC[1]], preferred_element_type=F32)
    ce_ref[0:hl, :] = jnp.where(pos == 0, 0.0, xbc[0:hl])
    ce_ref[hl:hl + tm, :] = xbc[hl:hl + tm]
    ce_ref[hl + tm:, :] = jnp.where(pos == tiles_per_seq - 1, 0.0, xbc[hl + tm:])

    def proj(cols):
        return jnp.dot(xb, w_ref[:, cols[0]:cols[1]], preferred_element_type=F32)

    qkv_ref[...] = proj(IN_QKV).astype(qkv_ref.dtype)
    z_ref[...] = proj(IN_Z)
    u = proj(IN_U)
    dt = _softplus(proj(IN_DT) + dtb_ref[...])
    dtt_ref[...] = dt.T[:2 * SSD_HEADS, :]

    pad = SSD_CONV // 2
    acc = cb_ref[...]
    for k in range(SSD_CONV):
        acc = acc + ce_ref[hl - pad + k:hl - pad + k + tm, :] * cw_ref[k:k + 1, :]
    xc = _silu(acc)
    xc_ref[...] = xc.astype(xc_ref.dtype)
    bt_ref[...] = xc[:, SSD_DIM:SSD_DIM + SSD_GROUPS * SSD_STATE].T.astype(bt_ref.dtype)

    n_slab = S5_DIM // LANES
    for k in range(n_slab):
        us_ref[k] = u[:, k * LANES:(k + 1) * LANES]
    for s in range(S5_CHUNK):
        for k in range(n_slab):
            u_ref[:, s * S5_DIM + k * LANES:s * S5_DIM + (k + 1) * LANES] = \
                us_ref[k, pl.ds(s, tm // S5_CHUNK, stride=S5_CHUNK), :]


def _inproj(x, g, b, w, conv_w, conv_b, dt_bias, t, apply_ln):
    n = x.shape[0]
    tm, hl = ROW_TILE, PROJ_HALO
    per = tm // hl
    row = lambda wd: pl.BlockSpec((tm, wd), lambda i: (i, 0))
    bn = SSD_GROUPS * SSD_STATE
    out_shape = [jax.ShapeDtypeStruct((n, 3 * NA_DIM), BF16), jax.ShapeDtypeStruct((n, SSD_DIM), F32),
                 jax.ShapeDtypeStruct((n, SSD_CONV_DIM), BF16), jax.ShapeDtypeStruct((bn, n), BF16),
                 jax.ShapeDtypeStruct((n // S5_CHUNK, S5_CHUNK * S5_DIM), F32),
                 jax.ShapeDtypeStruct((2 * SSD_HEADS, n), F32)]
    out_specs = [row(3 * NA_DIM), row(SSD_DIM), row(SSD_CONV_DIM), pl.BlockSpec((bn, tm), lambda i: (0, i)),
                 pl.BlockSpec((tm // S5_CHUNK, S5_CHUNK * S5_DIM), lambda i: (i, 0)),
                 pl.BlockSpec((2 * SSD_HEADS, tm), lambda i: (0, i))]
    if apply_ln:
        out_shape.append(jax.ShapeDtypeStruct((n, D_MODEL), F32))
        out_specs.append(row(D_MODEL))
    return pl.pallas_call(
        functools.partial(_inproj_kernel, apply_ln=apply_ln, tiles_per_seq=t // tm),
        grid=(n // tm,),
        in_specs=[row(D_MODEL),
                  pl.BlockSpec((hl, D_MODEL), lambda i: (jnp.maximum(i * per - 1, 0), 0)),
                  pl.BlockSpec((hl, D_MODEL), lambda i: (jnp.minimum((i + 1) * per, n // hl - 1), 0)),
                  _const_spec((1, D_MODEL)), _const_spec((1, D_MODEL)), _const_spec(w.shape),
                  _const_spec((SSD_CONV, SSD_CONV_DIM)), _const_spec((1, SSD_CONV_DIM)), _const_spec((1, DT_PAD))],
        out_specs=out_specs, out_shape=out_shape,
        scratch_shapes=[pltpu.VMEM((tm + 2 * hl, D_MODEL), BF16), pltpu.VMEM((tm + 2 * hl, SSD_CONV_DIM), F32),
                        pltpu.VMEM((S5_DIM // LANES, tm, LANES), F32)],
        compiler_params=_params("parallel"), name="in_proj",
    )(x, x, x, g, b, w, conv_w, conv_b, dt_bias)


def _na_bias_tables(rpb, rows):
    n_ri, n_ci = 2 * NA_WIN_ROWS - 1, 2 * NA_WIN_COLS - 1
    qr = np.arange(NA_Q_ROWS)
    kr = np.arange(NA_K_ROWS)
    c = np.arange(GRID_W)
    c0 = np.clip(c - NA_WIN_COLS // 2, 0, GRID_W - NA_WIN_COLS)
    col_ok = (c[None, :] >= c0[:, None]) & (c[None, :] < c0[:, None] + NA_WIN_COLS)
    ci = np.clip(c[None, :] - c[:, None], -(NA_WIN_COLS - 1), NA_WIN_COLS - 1) + (NA_WIN_COLS - 1)
    onehot_ci = (ci[None] == np.arange(n_ci)[:, None, None]).astype(np.float32)
    onehot_ri, ok = [], []
    for blk_row in (0, NA_Q_ROWS, rows - NA_Q_ROWS):
        start = min(max(blk_row - NA_WIN_ROWS // 2, 0), rows - NA_K_ROWS)
        r = blk_row + qr
        r0 = np.clip(r - NA_WIN_ROWS // 2, 0, rows - NA_WIN_ROWS)
        key_row = start + kr
        row_ok = (key_row[None, :] >= r0[:, None]) & (key_row[None, :] < r0[:, None] + NA_WIN_ROWS)
        ri = key_row[None, :] - r[:, None] + (NA_WIN_ROWS - 1)
        onehot_ri.append(((ri[..., None] == np.arange(n_ri)) & row_ok[..., None]).astype(np.float32))
        ok.append(row_ok[:, None, :, None] & col_ok[None, :, None, :])
    col_tab = jnp.einsum('hrc,cqk->hrqk', rpb, jnp.asarray(onehot_ci), precision=HIGHEST)
    bias = jnp.einsum('aqjr,hrwk->ahqwjk', jnp.asarray(np.stack(onehot_ri)), col_tab, precision=HIGHEST)
    bias = jnp.where(jnp.asarray(np.stack(ok))[:, None], bias, NEG_BIG)
    return bias.reshape(3, NA_HEADS, NA_Q_ROWS * GRID_W, NA_K_ROWS * GRID_W).astype(F32)


def _na_kernel(q_ref, k_ref, v_ref, bias_ref, o_ref, *, rows):
    i = pl.program_id(1)
    nblk = rows // NA_Q_ROWS
    case = jnp.where(i == 0, 0, jnp.where(i == nblk - 1, 2, 1))
    start_row = jnp.clip(i * NA_Q_ROWS - NA_WIN_ROWS // 2, 0, rows - NA_K_ROWS)
    start = pl.multiple_of(start_row * GRID_W, GRID_W)
    nk = NA_K_ROWS * GRID_W
    q = q_ref[...] * (HEAD_DIM ** -0.5)
    kw = k_ref[pl.ds(start, nk), :]
    vw = v_ref[pl.ds(start, nk), :]
    for h in range(NA_HEADS):
        sl = slice(h * HEAD_DIM, (h + 1) * HEAD_DIM)
        s = lax.dot_general(q[:, sl], kw[:, sl], (((1,), (1,)), ((), ())), preferred_element_type=F32)
        s = s + bias_ref[case, h]
        m = jnp.max(s, axis=-1, keepdims=True)
        p = jnp.exp(s - m)
        l = jnp.sum(p, axis=-1, keepdims=True)
        o = jnp.dot(p.astype(BF16), vw[:, sl], preferred_element_type=F32)
        o_ref[:, sl] = (o / l).astype(o_ref.dtype)


def _na(qkv, bias, bsz, t):
    rows = t // GRID_W
    tq = NA_Q_ROWS * GRID_W
    qkv3 = qkv.reshape(bsz, t, 3 * NA_DIM)
    out = pl.pallas_call(
        functools.partial(_na_kernel, rows=rows),
        grid=(bsz, rows // NA_Q_ROWS),
        in_specs=[pl.BlockSpec((None, tq, NA_DIM), lambda b, i: (b, i, 0)),
                  pl.BlockSpec((None, t, NA_DIM), lambda b, i: (b, 0, 1)),
                  pl.BlockSpec((None, t, NA_DIM), lambda b, i: (b, 0, 2)),
                  pl.BlockSpec(bias.shape, lambda b, i: (0, 0, 0, 0), pipeline_mode=pl.Buffered(1))],
        out_specs=pl.BlockSpec((None, tq, NA_DIM), lambda b, i: (b, i, 0)),
        out_shape=jax.ShapeDtypeStruct((bsz, t, NA_DIM), BF16),
        compiler_params=_params("parallel", "arbitrary"), name="na_attn",
    )(qkv3, qkv3, qkv3, bias)
    return out.reshape(bsz * t, NA_DIM)


def _tri(n, lower):
    r = lax.broadcasted_iota(jnp.int32, (n, n), 0)
    c = lax.broadcasted_iota(jnp.int32, (n, n), 1)
    return (r >= c) if lower else (r <= c)


def _split3(x):
    hi = x.astype(BF16).astype(F32)
    r = x - hi
    mid = r.astype(BF16).astype(F32)
    lo = (r - mid).astype(BF16).astype(F32)
    return hi, mid, lo


def _dot01_left(m01, x):
    return sum(jnp.dot(m01, p.astype(BF16), preferred_element_type=F32) for p in _split3(x))


def _dot01_right(x, m01):
    return sum(jnp.dot(p.astype(BF16), m01, preferred_element_type=F32) for p in _split3(x))


def _head_expand():
    r = lax.broadcasted_iota(jnp.int32, (SSD_HEADS, SSD_DIM), 0)
    c = lax.broadcasted_iota(jnp.int32, (SSD_HEADS, SSD_DIM), 1)
    return (c // HEAD_DIM == r).astype(BF16)


def _ssd_state_kernel(xf_ref, xb_ref, btf_ref, btb_ref, dttf_ref, dttb_ref, alog_ref,
                      pf_ref, pb_ref, sf_ref, sb_ref):
    q = SSD_CHUNK
    nh = SSD_HEADS
    pair_w = 2 * HEAD_DIM

    @pl.when(pl.program_id(1) == 0)
    def _():
        sf_ref[...] = jnp.zeros_like(sf_ref)
        sb_ref[...] = jnp.zeros_like(sb_ref)

    row_id = lax.broadcasted_iota(jnp.int32, (q, q), 0)
    col_id = lax.broadcasted_iota(jnp.int32, (q, q), 1)
    first_half = lax.broadcasted_iota(jnp.int32, (SSD_STATE, pair_w), 1) < HEAD_DIM
    a_col = -jnp.exp(alog_ref[...])
    for d, (x_ref, bt_ref, dtt_ref, prev_ref, state_ref) in enumerate(
            ((xf_ref, btf_ref, dttf_ref, pf_ref, sf_ref), (xb_ref, btb_ref, dttb_ref, pb_ref, sb_ref))):
        backward = d == 1
        hs = slice(d * nh, (d + 1) * nh)
        tri = ((row_id >= col_id) if backward else (row_id <= col_id)).astype(BF16)
        dtr = dtt_ref[hs, :]
        da = dtr * a_col[hs]
        local, decay = [], []
        da_rows = jnp.concatenate([da[:, cc * q:(cc + 1) * q] for cc in range(SSD_BLOCK_CHUNKS)], axis=0)
        cs_rows = _dot01_right(da_rows, tri)
        for cc in range(SSD_BLOCK_CHUNKS):
            rs = slice(cc * q, (cc + 1) * q)
            cs = cs_rows[cc * nh:(cc + 1) * nh]
            tot = cs[:, 0:1] if backward else cs[:, q - 1:q]
            w = jnp.exp(tot - cs) * dtr[:, rs]
            chunk_decay = jnp.exp(tot)
            local.append([])
            decay.append([])
            for pp in range(nh // 2):
                g = (2 * pp) // (nh // SSD_GROUPS)
                bt = bt_ref[g * SSD_STATE:(g + 1) * SSD_STATE, rs].astype(F32)
                xs = x_ref[rs, pp * pair_w:(pp + 1) * pair_w]
                lhs = jnp.concatenate([(bt * w[h:h + 1, :]).astype(BF16) for h in (2 * pp, 2 * pp + 1)], axis=0)
                both = jnp.dot(lhs, xs, preferred_element_type=F32)
                local[cc].append(jnp.where(first_half, both[:SSD_STATE], both[SSD_STATE:]))
                decay[cc].append(jnp.where(first_half, chunk_decay[2 * pp:2 * pp + 1, :],
                                           chunk_decay[2 * pp + 1:2 * pp + 2, :]))
        order = range(SSD_BLOCK_CHUNKS - 1, -1, -1) if backward else range(SSD_BLOCK_CHUNKS)
        for pp in range(nh // 2):
            lanes = slice(pp * pair_w, (pp + 1) * pair_w)
            state = state_ref[:, lanes]
            for cc in order:
                prev_ref[cc, :, lanes] = state.astype(prev_ref.dtype)
                state = state * decay[cc][pp] + local[cc][pp]
            state_ref[:, lanes] = state


def _ssd_states(xc3, bt, dtt, a_log16):
    bsz, t, _ = xc3.shape
    tb = SSD_BLOCK_CHUNKS * SSD_CHUNK
    nblk = t // tb
    bn = SSD_GROUPS * SSD_STATE
    fwd_rows = pl.BlockSpec((None, tb, SSD_CONV_DIM), lambda b, j: (b, j, 0))
    bwd_rows = pl.BlockSpec((None, tb, SSD_CONV_DIM), lambda b, j: (b, nblk - 1 - j, 0))
    fwd_cols = lambda rows: pl.BlockSpec((rows, tb), lambda b, j: (0, b * nblk + j))
    bwd_cols = lambda rows: pl.BlockSpec((rows, tb), lambda b, j: (0, b * nblk + nblk - 1 - j))
    state = jax.ShapeDtypeStruct((bsz, t // SSD_CHUNK, SSD_STATE, SSD_DIM), BF16)
    return pl.pallas_call(
        _ssd_state_kernel, grid=(bsz, nblk),
        in_specs=[fwd_rows, bwd_rows, fwd_cols(bn), bwd_cols(bn), fwd_cols(2 * SSD_HEADS), bwd_cols(2 * SSD_HEADS),
                  _const_spec((2 * SSD_HEADS, 1))],
        out_specs=[pl.BlockSpec((None, SSD_BLOCK_CHUNKS, SSD_STATE, SSD_DIM), lambda b, j: (b, j, 0, 0)),
                   pl.BlockSpec((None, SSD_BLOCK_CHUNKS, SSD_STATE, SSD_DIM),
                                lambda b, j: (b, nblk - 1 - j, 0, 0))],
        out_shape=[state, state],
        scratch_shapes=[pltpu.VMEM((SSD_STATE, SSD_DIM), F32), pltpu.VMEM((SSD_STATE, SSD_DIM), F32)],
        compiler_params=_params("parallel", "arbitrary"), name="ssd_state",
    )(xc3, xc3, bt, bt, dtt, dtt, a_log16.reshape(2 * SSD_HEADS, 1))


def _ssd_out_kernel(x_ref, z_ref, dtt_ref, pf_ref, pb_ref, alog_col_ref, dskip_ref, nw_ref, o_ref):
    q = SSD_CHUNK
    nh = SSD_HEADS
    row_id = lax.broadcasted_iota(jnp.int32, (q, q), 0)
    col_id = lax.broadcasted_iota(jnp.int32, (q, q), 1)
    lower = row_id >= col_id
    eye = row_id == col_id
    first_half = lax.broadcasted_iota(jnp.int32, (q, 2 * HEAD_DIM), 1) < HEAD_DIM
    tri_l = lower.astype(BF16)
    tri_u = (row_id <= col_id).astype(BF16)
    a_col = -jnp.exp(alog_col_ref[...])
    dtt_all = dtt_ref[...]
    da = dtt_all * a_col
    da_rows = jnp.concatenate([da[:, cc * q:(cc + 1) * q] for cc in range(SSD_BLOCK_CHUNKS)], axis=0)
    cs_rows_f, cs_rows_b = _dot01_right(da_rows, tri_u), _dot01_right(da_rows, tri_l)
    cs_cols_f, cs_cols_b = cs_rows_f.T, cs_rows_b.T
    e_cols_f, e_cols_b = jnp.exp(cs_cols_f), jnp.exp(cs_cols_b)
    for cc in range(SSD_BLOCK_CHUNKS):
        rs = slice(cc * q, (cc + 1) * q)
        dtr = dtt_all[:, rs]
        hrow = slice(cc * 2 * nh, (cc + 1) * 2 * nh)
        cs_row_f, cs_row_b = cs_rows_f[hrow], cs_rows_b[hrow]
        cs_col_f, cs_col_b = cs_cols_f[:, hrow], cs_cols_b[:, hrow]
        e_col_f, e_col_b = e_cols_f[:, hrow], e_cols_b[:, hrow]
        bc = x_ref[rs, SSD_DIM:SSD_CONV_DIM]
        groups = []
        for g in range(SSD_GROUPS):
            bg = bc[:, g * SSD_STATE:(g + 1) * SSD_STATE]
            cg = bc[:, (SSD_GROUPS + g) * SSD_STATE:(SSD_GROUPS + g + 1) * SSD_STATE]
            cb = lax.dot_general(cg, bg, (((1,), (1,)), ((), ())), preferred_element_type=F32)
            c2 = jnp.concatenate([cg, cg], axis=1).astype(F32)
            pairs = []
            for pp in range(SSD_HEADS // SSD_GROUPS // 2):
                h0 = g * (SSD_HEADS // SSD_GROUPS) + 2 * pp
                lanes = slice(h0 * HEAD_DIM, (h0 + 2) * HEAD_DIM)
                rhs = jnp.concatenate([x_ref[rs, lanes], pf_ref[cc, :, lanes], pb_ref[cc, :, lanes]], axis=0)
                lhs = []
                for h in (h0, h0 + 1):
                    seg_f = cs_col_f[:, h:h + 1] - cs_row_f[h:h + 1, :]
                    seg_b = cs_col_b[:, nh + h:nh + h + 1] - cs_row_b[nh + h:nh + h + 1, :]
                    dt_f, dt_b = dtr[h:h + 1, :], dtr[nh + h:nh + h + 1, :]
                    lmat = (jnp.exp(jnp.where(lower, seg_f, seg_b)) * jnp.where(lower, dt_f, dt_b)
                            + jnp.where(eye, dt_b, 0.0))
                    e2 = jnp.where(first_half, e_col_f[:, h:h + 1], e_col_b[:, nh + h:nh + h + 1])
                    lhs.append(jnp.concatenate([(cb * lmat).astype(BF16), (c2 * e2).astype(BF16)], axis=1))
                both = jnp.dot(jnp.concatenate(lhs, axis=0), rhs, preferred_element_type=F32)
                pairs.append(jnp.where(first_half, both[:q], both[q:]))
            groups.append(jnp.concatenate(pairs, axis=1))
        y = jnp.concatenate(groups, axis=1) + dskip_ref[...] * x_ref[rs, :SSD_DIM].astype(F32)
        yg = y * _silu(z_ref[rs, :])
        ms = jnp.mean(yg * yg, axis=-1, keepdims=True)
        o_ref[rs, :] = (yg * lax.rsqrt(ms + LN_EPS) * nw_ref[...]).astype(o_ref.dtype)


def _ssd_out(xc3, z3, dtt, prev_f, prev_b, a_log16, d_skip, norm_w):
    bsz, t, _ = xc3.shape
    tb = SSD_BLOCK_CHUNKS * SSD_CHUNK
    nblk = t // tb
    blk = lambda wd: pl.BlockSpec((None, tb, wd), lambda b, j: (b, j, 0))
    prev = pl.BlockSpec((None, SSD_BLOCK_CHUNKS, SSD_STATE, SSD_DIM), lambda b, j: (b, j, 0, 0))
    out = pl.pallas_call(
        _ssd_out_kernel, grid=(bsz, nblk),
        in_specs=[blk(SSD_CONV_DIM), blk(SSD_DIM),
                  pl.BlockSpec((2 * SSD_HEADS, tb), lambda b, j: (0, b * nblk + j)),
                  prev, prev, _const_spec((2 * SSD_HEADS, 1)),
                  _const_spec((1, SSD_DIM)), _const_spec((1, SSD_DIM))],
        out_specs=blk(SSD_DIM),
        out_shape=jax.ShapeDtypeStruct((bsz, t, SSD_DIM), BF16),
        compiler_params=_params("parallel", "parallel"), name="ssd_out",
    )(xc3, z3, dtt, prev_f, prev_b, a_log16.reshape(2 * SSD_HEADS, 1), d_skip, norm_w)
    return out.reshape(bsz * t, SSD_DIM)


def _s5_tables(lam_re, lam_im, log_dt, b_re, b_im, c_re, c_im):
    lc = S5_CHUNK
    hp = HIGHEST
    pw_re, pw_im, bb_re, bb_im = [], [], [], []
    for d in range(2):
        dt = jnp.exp(log_dt[d])[:, None]
        lr, li = lam_re[d], lam_im[d]
        mag = jnp.exp(lr * dt)
        ar, ai = mag * jnp.cos(li * dt), mag * jnp.sin(li * dt)
        den = lr * lr + li * li
        fr = ((ar - 1.0) * lr + ai * li) / den
        fi = (ai * lr - (ar - 1.0) * li) / den
        bb_re.append(fr[..., None] * b_re[d] - fi[..., None] * b_im[d])
        bb_im.append(fr[..., None] * b_im[d] + fi[..., None] * b_re[d])
        pr, pi = [jnp.ones_like(ar)], [jnp.zeros_like(ar)]
        for _ in range(lc):
            pr, pi = pr + [pr[-1] * ar - pi[-1] * ai], pi + [pr[-1] * ai + pi[-1] * ar]
        pw_re.append(jnp.stack(pr))
        pw_im.append(jnp.stack(pi))

    def kern(d):
        wr = pw_re[d][:lc, :, :, None] * bb_re[d][None] - pw_im[d][:lc, :, :, None] * bb_im[d][None]
        wi = pw_re[d][:lc, :, :, None] * bb_im[d][None] + pw_im[d][:lc, :, :, None] * bb_re[d][None]
        return (jnp.einsum('ghp,kgpj->kghj', c_re[d], wr, precision=hp)
                - jnp.einsum('ghp,kgpj->kghj', c_im[d], wi, precision=hp))

    kf, kb = kern(0), kern(1)
    l = jnp.arange(lc)
    lag = l[:, None] - l[None, :]
    tf = jnp.where((lag >= 0)[:, :, None, None, None], kf[jnp.clip(lag, 0, lc - 1)], 0.0)
    tb = jnp.where((lag <= 0)[:, :, None, None, None], kb[jnp.clip(-lag, 0, lc - 1)], 0.0)
    toep = (tf + tb).transpose(2, 1, 4, 0, 3).reshape(S5_GROUPS, lc * S5_GROUP_CH, lc * S5_GROUP_CH)

    def state_in(d, powers):
        wr = pw_re[d][powers][:, :, :, None] * bb_re[d][None] - pw_im[d][powers][:, :, :, None] * bb_im[d][None]
        wi = pw_re[d][powers][:, :, :, None] * bb_im[d][None] + pw_im[d][powers][:, :, :, None] * bb_re[d][None]
        to_rows = lambda w: w.transpose(1, 0, 3, 2).reshape(S5_GROUPS, lc * S5_GROUP_CH, S5_STATE)
        return to_rows(wr), to_rows(wi)

    f_re, f_im = state_in(0, lc - 1 - l)
    b_re_, b_im_ = state_in(1, l)
    m_state = jnp.concatenate([f_re, b_re_, f_im, b_im_], axis=-1)

    def state_out(d, powers):
        cpr = c_re[d][None] * pw_re[d][powers][:, :, None, :] - c_im[d][None] * pw_im[d][powers][:, :, None, :]
        cpi = c_re[d][None] * pw_im[d][powers][:, :, None, :] + c_im[d][None] * pw_re[d][powers][:, :, None, :]
        to_cols = lambda w: w.transpose(1, 3, 0, 2).reshape(S5_GROUPS, S5_STATE, lc * S5_GROUP_CH)
        return to_cols(cpr), -to_cols(cpi)

    of_re, of_im = state_out(0, l + 1)
    ob_re, ob_im = state_out(1, lc - l)
    m_off = jnp.concatenate([of_re, ob_re, of_im, ob_im], axis=1)
    dec_re = jnp.concatenate([pw_re[0][lc], pw_re[1][lc]], axis=-1)[:, None, :]
    dec_im = jnp.concatenate([pw_im[0][lc], pw_im[1][lc]], axis=-1)[:, None, :]
    return toep.astype(BF16), m_state.astype(BF16), m_off.astype(BF16), dec_re, dec_im


S5_W = S5_CHUNK * S5_GROUP_CH
S5_RELAYOUT_CHUNKS = 64


def _s5_group_kernel(u_ref, o_ref):
    x = u_ref[...]
    for g in range(S5_GROUPS):
        lo = g * S5_GROUP_CH
        pieces = [x[:, s * S5_DIM + lo:s * S5_DIM + lo + S5_GROUP_CH] for s in range(S5_CHUNK)]
        o_ref[g] = jnp.concatenate(pieces, axis=-1).astype(o_ref.dtype)


def _s5_ungroup_kernel(y_ref, u_ref, d_ref, o_ref):
    ys = [y_ref[g] for g in range(S5_GROUPS)]
    for l in range(S5_CHUNK):
        cols = slice(l * S5_DIM, (l + 1) * S5_DIM)
        pieces = [y[:, l * S5_GROUP_CH:(l + 1) * S5_GROUP_CH] for y in ys]
        o_ref[:, cols] = jnp.concatenate(pieces, axis=-1) + d_ref[...] * u_ref[:, cols]


def _s5_kernel(u_ref, toep_ref, mst_ref, moff_ref, are_ref, aim_ref, y_ref, s_ref, e_ref, *, nc, bsz):
    ub = u_ref[...]
    y_ref[...] = jnp.dot(ub, toep_ref[...], preferred_element_type=F32)
    s_ref[...] = jnp.dot(ub, mst_ref[...], preferred_element_type=F32)
    half = S5_STATE
    ar = jnp.broadcast_to(are_ref[...], (bsz, 2 * half))
    ai = jnp.broadcast_to(aim_ref[...], (bsz, 2 * half))
    is_fwd = lax.broadcasted_iota(jnp.int32, (bsz, 2 * half), 1) < half

    def body(i, carry):
        er, ei = carry
        rf = pl.multiple_of(i * bsz, bsz)
        rb = pl.multiple_of((nc - 1 - i) * bsz, bsz)
        e_ref[pl.ds(rf, bsz), 0:half] = er[:, :half]
        e_ref[pl.ds(rb, bsz), half:2 * half] = er[:, half:]
        e_ref[pl.ds(rf, bsz), 2 * half:3 * half] = ei[:, :half]
        e_ref[pl.ds(rb, bsz), 3 * half:4 * half] = ei[:, half:]
        sf = s_ref[pl.ds(rf, bsz), :]
        sb = s_ref[pl.ds(rb, bsz), :]
        sr = jnp.where(is_fwd, sf[:, :2 * half], sb[:, :2 * half])
        si = jnp.where(is_fwd, sf[:, 2 * half:], sb[:, 2 * half:])
        return ar * er - ai * ei + sr, ar * ei + ai * er + si

    zero = jnp.zeros((bsz, 2 * half), F32)
    lax.fori_loop(0, nc, body, (zero, zero))
    y_ref[...] += jnp.dot(e_ref[...].astype(BF16), moff_ref[...], preferred_element_type=F32)


def _s5(u_cm, tables, d_skip, bsz, t):
    toep, m_state, m_off, dec_re, dec_im = tables
    nc = t // S5_CHUNK
    w = S5_W
    rc = min(S5_RELAYOUT_CHUNKS, nc)
    u3 = u_cm.reshape(bsz, nc, S5_CHUNK * S5_DIM)
    tok = pl.BlockSpec((None, rc, S5_CHUNK * S5_DIM), lambda b, j: (b, j, 0))
    grouped = pl.BlockSpec((S5_GROUPS, rc, w), lambda b, j: (0, j, b))
    ug = pl.pallas_call(
        _s5_group_kernel, grid=(bsz, nc // rc), in_specs=[tok], out_specs=grouped,
        out_shape=jax.ShapeDtypeStruct((S5_GROUPS, nc, bsz * w), BF16),
        compiler_params=_params("parallel", "parallel"), name="s5_group",
    )(u3)
    grp = lambda shape: pl.BlockSpec((None,) + shape, lambda g: (g, 0, 0))
    yg = pl.pallas_call(
        functools.partial(_s5_kernel, nc=nc, bsz=bsz),
        grid=(S5_GROUPS,),
        in_specs=[grp((nc * bsz, w)), grp((w, w)), grp((w, w)), grp((w, w)),
                  grp((1, 2 * S5_STATE)), grp((1, 2 * S5_STATE))],
        out_specs=grp((nc * bsz, w)),
        out_shape=jax.ShapeDtypeStruct((S5_GROUPS, nc * bsz, w), F32),
        scratch_shapes=[pltpu.VMEM((nc * bsz, w), F32), pltpu.VMEM((nc * bsz, w), F32)],
        compiler_params=_params("parallel"), name="s5_scan",
    )(ug.reshape(S5_GROUPS, nc * bsz, w), toep, m_state, m_off, dec_re, dec_im)
    y = pl.pallas_call(
        _s5_ungroup_kernel, grid=(bsz, nc // rc),
        in_specs=[grouped, tok, pl.BlockSpec((1, S5_DIM), lambda b, j: (0, 0))], out_specs=tok,
        out_shape=jax.ShapeDtypeStruct((bsz, nc, S5_CHUNK * S5_DIM), F32),
        compiler_params=_params("parallel", "parallel"), name="s5_ungroup",
    )(yg.reshape(S5_GROUPS, nc, bsz * w), u3, d_skip)
    return y.reshape(bsz * nc, S5_CHUNK * S5_DIM)


def _mixout_kernel(h_ref, na_ref, ssd_ref, y5_ref, gw_ref, gb_ref, wo_ref, g_ref, b_ref, o_ref, ys_ref):
    n_slab = S5_DIM // LANES
    for l in range(S5_CHUNK):
        for k in range(n_slab):
            ys_ref[k, pl.ds(l, ROW_TILE // S5_CHUNK, stride=S5_CHUNK), :] = \
                y5_ref[:, l * S5_DIM + k * LANES:l * S5_DIM + (k + 1) * LANES]
    y5 = jnp.concatenate([ys_ref[k] for k in range(n_slab)], axis=-1)
    gl = 0.5 * y5 * (1.0 + jnp.tanh(math.sqrt(2.0 / math.pi) * (y5 + 0.044715 * (y5 * y5 * y5))))
    gate = jnp.dot(gl.astype(BF16), gw_ref[...], preferred_element_type=F32) + gb_ref[...]
    o5 = gl * _sigmoid(gate)
    mix = jnp.dot(na_ref[...], wo_ref[0:NA_DIM, :], preferred_element_type=F32)
    mix += jnp.dot(ssd_ref[...], wo_ref[NA_DIM:NA_DIM + SSD_DIM, :], preferred_element_type=F32)
    mix += jnp.dot(o5.astype(BF16), wo_ref[NA_DIM + SSD_DIM:, :], preferred_element_type=F32)
    o_ref[...] = _layernorm(DEEPNORM_ALPHA * h_ref[...] + mix, g_ref[...], b_ref[...])


def _mixout(h, o_na, o_ssd, y5_cm, glu_w, glu_b, w_out, g, b):
    n = h.shape[0]
    row = lambda wd: pl.BlockSpec((ROW_TILE, wd), lambda i: (i, 0))
    return pl.pallas_call(
        _mixout_kernel, grid=(n // ROW_TILE,),
        in_specs=[row(D_MODEL), row(NA_DIM), row(SSD_DIM),
                  pl.BlockSpec((ROW_TILE // S5_CHUNK, S5_CHUNK * S5_DIM), lambda i: (i, 0)),
                  _const_spec((S5_DIM, S5_DIM)), _const_spec((1, S5_DIM)),
                  _const_spec((D_MODEL, D_MODEL)), _const_spec((1, D_MODEL)), _const_spec((1, D_MODEL))],
        out_specs=row(D_MODEL), out_shape=jax.ShapeDtypeStruct((n, D_MODEL), F32),
        scratch_shapes=[pltpu.VMEM((S5_DIM // LANES, ROW_TILE, LANES), F32)],
        compiler_params=_params("parallel"), name="mix_out",
    )(h, o_na, o_ssd, y5_cm, glu_w, glu_b, w_out, g, b)


def _kvproj_kernel(m_ref, wk_ref, wv_ref, k_ref, v_ref):
    mb = m_ref[...].astype(BF16)
    k_ref[...] = jnp.dot(mb, wk_ref[...], preferred_element_type=F32).astype(k_ref.dtype)
    v_ref[...] = jnp.dot(mb, wv_ref[...], preferred_element_type=F32).astype(v_ref.dtype)


def _kvproj(mem2, wk, wv):
    n = mem2.shape[0]
    tm = min(ROW_TILE, n)
    row = pl.BlockSpec((tm, D_MODEL), lambda i: (i, 0))
    return pl.pallas_call(
        _kvproj_kernel, grid=(n // tm,),
        in_specs=[row, _const_spec((D_MODEL, D_MODEL)), _const_spec((D_MODEL, D_MODEL))],
        out_specs=[row, row], out_shape=[jax.ShapeDtypeStruct((n, D_MODEL), BF16)] * 2,
        compiler_params=_params("parallel"), name="xa_kv_proj",
    )(mem2, wk, wv)


def _xattn_kernel(h_ref, k_ref, v_ref, wq_ref, wo_ref, g_ref, b_ref, o_ref):
    h = h_ref[...]
    q = jnp.dot(h.astype(BF16), wq_ref[...], preferred_element_type=F32) * (XA_HEAD_DIM ** -0.5)
    qb = q.astype(BF16)
    xa = None
    for hd in range(XA_HEADS):
        sl = slice(hd * XA_HEAD_DIM, (hd + 1) * XA_HEAD_DIM)
        s = lax.dot_general(qb[:, sl], k_ref[:, sl], (((1,), (1,)), ((), ())), preferred_element_type=F32)
        m = jnp.max(s, axis=-1, keepdims=True)
        p = jnp.exp(s - m)
        l = jnp.sum(p, axis=-1, keepdims=True)
        o = jnp.dot(p.astype(BF16), v_ref[:, sl], preferred_element_type=F32) / l
        part = jnp.dot(o.astype(BF16), wo_ref[sl, :], preferred_element_type=F32)
        xa = part if xa is None else xa + part
    o_ref[...] = _layernorm(DEEPNORM_ALPHA * h + xa, g_ref[...], b_ref[...])


def _xattn(h, k3, v3, wq, wo, g, b, bsz, t):
    m = k3.shape[1]
    h3 = h.reshape(bsz, t, D_MODEL)
    row = pl.BlockSpec((None, ROW_TILE, D_MODEL), lambda bb, i: (bb, i, 0))
    kv = pl.BlockSpec((None, m, D_MODEL), lambda bb, i: (bb, 0, 0))
    const = lambda shape: pl.BlockSpec(shape, lambda bb, i: (0, 0))
    out = pl.pallas_call(
        _xattn_kernel, grid=(bsz, t // ROW_TILE),
        in_specs=[row, kv, kv, const((D_MODEL, D_MODEL)), const((D_MODEL, D_MODEL)),
                  const((1, D_MODEL)), const((1, D_MODEL))],
        out_specs=row, out_shape=jax.ShapeDtypeStruct((bsz, t, D_MODEL), F32),
        compiler_params=_params("parallel", "parallel"), name="cross_attn",
    )(h3, k3, v3, wq, wo, g, b)
    return out.reshape(bsz * t, D_MODEL)


FF_TILE = 1024


def _mlp_kernel(h_ref, w1_ref, w2_ref, g_ref, b_ref, o_ref):
    h = h_ref[...]
    hb = h.astype(BF16)
    ff = None
    for c in range(D_FF // FF_TILE):
        sl = slice(c * FF_TILE, (c + 1) * FF_TILE)
        a = jnp.maximum(jnp.dot(hb, w1_ref[:, sl], preferred_element_type=F32), 0.0)
        part = jnp.dot((a * a).astype(BF16), w2_ref[sl, :], preferred_element_type=F32)
        ff = part if ff is None else ff + part
    o_ref[...] = _layernorm(DEEPNORM_ALPHA * h + ff, g_ref[...], b_ref[...])


def _mlp(h, w1, w2, g, b):
    n = h.shape[0]
    row = pl.BlockSpec((ROW_TILE, D_MODEL), lambda i: (i, 0))
    return pl.pallas_call(
        _mlp_kernel, grid=(n // ROW_TILE,),
        in_specs=[row,
                  pl.BlockSpec((D_MODEL, D_FF), lambda i: (0, 0), pipeline_mode=pl.Buffered(1)),
                  pl.BlockSpec((D_FF, D_MODEL), lambda i: (0, 0), pipeline_mode=pl.Buffered(1)),
                  _const_spec((1, D_MODEL)), _const_spec((1, D_MODEL))],
        out_specs=row, out_shape=jax.ShapeDtypeStruct((n, D_MODEL), F32),
        compiler_params=_params("parallel"), name="mlp",
    )(h, w1, w2, g, b)


def _permute_w_in(w):
    dt_lo = 3 * NA_DIM + SSD_DIM + SSD_CONV_DIM
    dt_hi = dt_lo + 2 * SSD_HEADS
    pad = jnp.zeros((w.shape[0], DT_PAD - 2 * SSD_HEADS), w.dtype)
    return jnp.concatenate([w[:, :dt_lo], w[:, dt_hi:], w[:, dt_lo:dt_hi], pad], axis=1).astype(BF16)


def kernel(x, mem, ln_in_g, ln_in_b, w_in, na_rpb, ssd_conv_w, ssd_conv_b, ssd_dt_bias, ssd_a_log, ssd_d,
           ssd_norm_w, s5_lam_re, s5_lam_im, s5_log_dt, s5_b_re, s5_b_im, s5_c_re, s5_c_im, s5_d, s5_glu_w,
           s5_glu_b, w_mix_out, ln_mix_g, ln_mix_b, xa_wq, xa_wk, xa_wv, xa_wo, ln_xa_g, ln_xa_b, mlp_w1,
           mlp_w2, ln_mlp_g, ln_mlp_b):
    bsz, t, _ = x.shape
    n = bsz * t
    rows = t // GRID_W
    assert n % ROW_TILE == 0 and t % (SSD_BLOCK_CHUNKS * SSD_CHUNK) == 0 and rows % NA_Q_ROWS == 0
    assert rows >= NA_K_ROWS and t % S5_CHUNK == 0
    row1 = lambda v: v.reshape(1, -1).astype(F32)
    mem2 = mem.reshape(-1, D_MODEL)
    h = x.reshape(n, D_MODEL)
    for l in range(DEPTH):
        dt_bias = jnp.pad(row1(ssd_dt_bias[l]), ((0, 0), (0, DT_PAD - 2 * SSD_HEADS)))
        outs = _inproj(h, row1(ln_in_g), row1(ln_in_b), _permute_w_in(w_in[l]), ssd_conv_w[l].astype(F32),
                       row1(ssd_conv_b[l]), dt_bias, t, apply_ln=(l == 0))
        qkv, z, xc, bt, u_cm, dtt = outs[:6]
        if l == 0:
            h = outs[6]
        o_na = _na(qkv, _na_bias_tables(na_rpb[l].astype(F32), rows), bsz, t)

        xc3 = xc.reshape(bsz, t, SSD_CONV_DIM)
        a_log16 = row1(ssd_a_log[l])
        prev_f, prev_b = _ssd_states(xc3, bt, dtt, a_log16)
        o_ssd = _ssd_out(xc3, z.reshape(bsz, t, SSD_DIM), dtt, prev_f, prev_b, a_log16,
                         row1(jnp.repeat(ssd_d[l], HEAD_DIM)), row1(ssd_norm_w[l]))

        tables = _s5_tables(s5_lam_re[l].astype(F32), s5_lam_im[l].astype(F32), s5_log_dt[l].astype(F32),
                            s5_b_re[l].astype(F32), s5_b_im[l].astype(F32), s5_c_re[l].astype(F32),
                            s5_c_im[l].astype(F32))
        y5 = _s5(u_cm, tables, row1(s5_d[l]), bsz, t)

        h = _mixout(h, o_na, o_ssd, y5, s5_glu_w[l].astype(BF16), row1(s5_glu_b[l]),
                    w_mix_out[l].astype(BF16), row1(ln_mix_g[l]), row1(ln_mix_b[l]))
        k2, v2 = _kvproj(mem2, xa_wk[l].astype(BF16), xa_wv[l].astype(BF16))
        h = _xattn(h, k2.reshape(bsz, -1, D_MODEL), v2.reshape(bsz, -1, D_MODEL), xa_wq[l].astype(BF16),
                   xa_wo[l].astype(BF16), row1(ln_xa_g[l]), row1(ln_xa_b[l]), bsz, t)
        h = _mlp(h, mlp_w1[l].astype(BF16), mlp_w2[l].astype(BF16), row1(ln_mlp_g[l]), row1(ln_mlp_b[l]))
    return h.reshape(bsz, t, D_MODEL)
```

```python
import functools
import math

import jax
import jax.numpy as jnp
import numpy as np
from jax import lax
from jax.experimental import pallas as pl
from jax.experimental.pallas import tpu as pltpu

F32 = jnp.float32
BF16 = jnp.bfloat16
HIGHEST = lax.Precision.HIGHEST

D_MODEL = 1024
DEPTH = 2
GRID_W = 64
HEAD_DIM = 64
NA_DIM = 256
NA_HEADS = 4
NA_WIN_ROWS = 8
NA_WIN_COLS = 16
SSD_DIM = 512
SSD_HEADS = 8
SSD_GROUPS = 2
SSD_STATE = 64
SSD_CONV = 5
SSD_CHUNK = 128
SSD_CONV_DIM = SSD_DIM + 2 * SSD_GROUPS * SSD_STATE
S5_DIM = 256
S5_GROUP_CH = 16
S5_GROUPS = 16
S5_STATE = 64
XA_HEADS = 4
XA_HEAD_DIM = 256
D_FF = 4096
LN_EPS = 1e-5
NEG_BIG = -1e30
DEEPNORM_ALPHA = (2 * DEPTH) ** 0.25

ROW_TILE = 512
XA_ROW_TILE = 1024
MIX_ROW_TILE = 1024
MLP_ROW_TILE = 1024
IN_ROW_TILE = 1024
NA_Q_ROWS = 4
NA_K_ROWS = NA_Q_ROWS + NA_WIN_ROWS
NA_BLOCKS_PER_STEP = 4
SSD_BLOCK_CHUNKS = 4
S5_CHUNK = 16
LANES = 128
DT_PAD = LANES
VMEM_LIMIT = 56 * 1024 * 1024


def _params(*sem):
    return pltpu.CompilerParams(dimension_semantics=sem, vmem_limit_bytes=VMEM_LIMIT)


def _layernorm(x, g, b):
    mu = jnp.mean(x, axis=-1, keepdims=True)
    xc = x - mu
    var = jnp.mean(xc * xc, axis=-1, keepdims=True)
    return xc * lax.rsqrt(var + LN_EPS) * g + b


def _sigmoid(x):
    return 0.5 + 0.5 * jnp.tanh(0.5 * x)


def _silu(x):
    return x * _sigmoid(x)


def _softplus(x):
    return jnp.maximum(x, 0.0) + jnp.log1p(jnp.exp(-jnp.abs(x)))


def _const_spec(shape):
    n = len(shape)
    return pl.BlockSpec(shape, lambda *_: (0,) * n)


IN_QKV, IN_Z, IN_XBC, IN_U, IN_DT = (0, 768), (768, 1280), (1280, 2048), (2048, 2304), (2304, 2304 + DT_PAD)
PROJ_HALO = 16


def _inproj_kernel(x_ref, xp_ref, xn_ref, g_ref, b_ref, w_ref, cw_ref, cb_ref, dtb_ref,
                   qkv_ref, z_ref, xc_ref, bt_ref, u_ref, dtt_ref, *rest, apply_ln, tiles_per_seq):
    xe_ref, ce_ref, us_ref = rest[-3:]
    tm, hl = IN_ROW_TILE, PROJ_HALO
    pos = pl.program_id(0) % tiles_per_seq
    x, xp, xn = x_ref[...], xp_ref[...], xn_ref[...]
    if apply_ln:
        x = _layernorm(x, g_ref[...], b_ref[...])
        xp = _layernorm(xp, g_ref[...], b_ref[...])
        xn = _layernorm(xn, g_ref[...], b_ref[...])
        rest[0][...] = x
    xe_ref[0:hl, :] = xp.astype(BF16)
    xe_ref[hl:hl + tm, :] = x.astype(BF16)
    xe_ref[hl + tm:, :] = xn.astype(BF16)

    half = tm // 2
    pad = SSD_CONV // 2
    n_slab = S5_DIM // LANES
    for r in range(2):
        rs = slice(r * half, (r + 1) * half)
        xbc = jnp.dot(xe_ref[r * half:r * half + half + 2 * hl, :], w_ref[:, IN_XBC[0]:IN_XBC[1]],
                      preferred_element_type=F32)
        before = jnp.where(pos == 0, 0.0, xbc[0:hl]) if r == 0 else xbc[0:hl]
        after = jnp.where(pos == tiles_per_seq - 1, 0.0, xbc[hl + half:]) if r == 1 else xbc[hl + half:]
        ce_ref[r, 0:hl, :] = before
        ce_ref[r, hl:hl + half, :] = xbc[hl:hl + half]
        ce_ref[r, hl + half:, :] = after

        xb = xe_ref[hl + r * half:hl + (r + 1) * half, :]

        def proj(cols):
            return jnp.dot(xb, w_ref[:, cols[0]:cols[1]], preferred_element_type=F32)

        qkv_ref[rs, :] = proj(IN_QKV).astype(qkv_ref.dtype)
        z_ref[rs, :] = proj(IN_Z)
        u = proj(IN_U)
        dt = _softplus(proj(IN_DT) + dtb_ref[...])
        dtt_ref[:, rs] = dt.T[:2 * SSD_HEADS, :]

        acc = cb_ref[...]
        for k in range(SSD_CONV):
            acc = acc + ce_ref[r, hl - pad + k:hl - pad + k + half, :] * cw_ref[k:k + 1, :]
        xc = _silu(acc)
        xc_ref[rs, :] = xc.astype(xc_ref.dtype)
        bt_ref[:, rs] = xc[:, SSD_DIM:SSD_DIM + SSD_GROUPS * SSD_STATE].T.astype(bt_ref.dtype)

        cpr = half // S5_CHUNK
        for k in range(n_slab):
            us_ref[r, k] = u[:, k * LANES:(k + 1) * LANES]
        for s in range(S5_CHUNK):
            for k in range(n_slab):
                u_ref[r * cpr:(r + 1) * cpr, s * S5_DIM + k * LANES:s * S5_DIM + (k + 1) * LANES] = \
                    us_ref[r, k, pl.ds(s, cpr, stride=S5_CHUNK), :]


def _inproj(x, g, b, w, conv_w, conv_b, dt_bias, t, apply_ln):
    n = x.shape[0]
    tm, hl = IN_ROW_TILE, PROJ_HALO
    per = tm // hl
    row = lambda wd: pl.BlockSpec((tm, wd), lambda i: (i, 0))
    bn = SSD_GROUPS * SSD_STATE
    out_shape = [jax.ShapeDtypeStruct((n, 3 * NA_DIM), BF16), jax.ShapeDtypeStruct((n, SSD_DIM), F32),
                 jax.ShapeDtypeStruct((n, SSD_CONV_DIM), BF16), jax.ShapeDtypeStruct((bn, n), BF16),
                 jax.ShapeDtypeStruct((n // S5_CHUNK, S5_CHUNK * S5_DIM), F32),
                 jax.ShapeDtypeStruct((2 * SSD_HEADS, n), F32)]
    out_specs = [row(3 * NA_DIM), row(SSD_DIM), row(SSD_CONV_DIM), pl.BlockSpec((bn, tm), lambda i: (0, i)),
                 pl.BlockSpec((tm // S5_CHUNK, S5_CHUNK * S5_DIM), lambda i: (i, 0)),
                 pl.BlockSpec((2 * SSD_HEADS, tm), lambda i: (0, i))]
    if apply_ln:
        out_shape.append(jax.ShapeDtypeStruct((n, D_MODEL), F32))
        out_specs.append(row(D_MODEL))
    return pl.pallas_call(
        functools.partial(_inproj_kernel, apply_ln=apply_ln, tiles_per_seq=t // tm),
        grid=(n // tm,),
        in_specs=[row(D_MODEL),
                  pl.BlockSpec((hl, D_MODEL), lambda i: (jnp.maximum(i * per - 1, 0), 0)),
                  pl.BlockSpec((hl, D_MODEL), lambda i: (jnp.minimum((i + 1) * per, n // hl - 1), 0)),
                  _const_spec((1, D_MODEL)), _const_spec((1, D_MODEL)), _const_spec(w.shape),
                  _const_spec((SSD_CONV, SSD_CONV_DIM)), _const_spec((1, SSD_CONV_DIM)), _const_spec((1, DT_PAD))],
        out_specs=out_specs, out_shape=out_shape,
        scratch_shapes=[pltpu.VMEM((tm + 2 * hl, D_MODEL), BF16),
                        pltpu.VMEM((2, tm // 2 + 2 * hl, SSD_CONV_DIM), F32),
                        pltpu.VMEM((2, S5_DIM // LANES, tm // 2, LANES), F32)],
        compiler_params=_params("parallel"), name="in_proj",
    )(x, x, x, g, b, w, conv_w, conv_b, dt_bias)


def _na_bias_tables(rpb, rows):
    n_ri, n_ci = 2 * NA_WIN_ROWS - 1, 2 * NA_WIN_COLS - 1
    qr = np.arange(NA_Q_ROWS)
    kr = np.arange(NA_K_ROWS)
    c = np.arange(GRID_W)
    c0 = np.clip(c - NA_WIN_COLS // 2, 0, GRID_W - NA_WIN_COLS)
    col_ok = (c[None, :] >= c0[:, None]) & (c[None, :] < c0[:, None] + NA_WIN_COLS)
    ci = np.clip(c[None, :] - c[:, None], -(NA_WIN_COLS - 1), NA_WIN_COLS - 1) + (NA_WIN_COLS - 1)
    onehot_ci = (ci[None] == np.arange(n_ci)[:, None, None]).astype(np.float32)
    onehot_ri, ok = [], []
    for blk_row in (0, NA_Q_ROWS, rows - NA_Q_ROWS):
        start = min(max(blk_row - NA_WIN_ROWS // 2, 0), rows - NA_K_ROWS)
        r = blk_row + qr
        r0 = np.clip(r - NA_WIN_ROWS // 2, 0, rows - NA_WIN_ROWS)
        key_row = start + kr
        row_ok = (key_row[None, :] >= r0[:, None]) & (key_row[None, :] < r0[:, None] + NA_WIN_ROWS)
        ri = key_row[None, :] - r[:, None] + (NA_WIN_ROWS - 1)
        onehot_ri.append(((ri[..., None] == np.arange(n_ri)) & row_ok[..., None]).astype(np.float32))
        ok.append(row_ok[:, None, :, None] & col_ok[None, :, None, :])
    col_tab = jnp.einsum('hrc,cqk->hrqk', rpb, jnp.asarray(onehot_ci), precision=HIGHEST)
    bias = jnp.einsum('aqjr,hrwk->ahqwjk', jnp.asarray(np.stack(onehot_ri)), col_tab, precision=HIGHEST)
    bias = jnp.where(jnp.asarray(np.stack(ok))[:, None], bias, NEG_BIG)
    return bias.reshape(3, NA_HEADS, NA_Q_ROWS * GRID_W, NA_K_ROWS * GRID_W).astype(F32)


def _na_kernel(q_ref, k_ref, v_ref, bias_ref, o_ref, *, rows):
    nblk = rows // NA_Q_ROWS
    nk = NA_K_ROWS * GRID_W
    tq = NA_Q_ROWS * GRID_W
    for r in range(NA_BLOCKS_PER_STEP):
        i = pl.program_id(1) * NA_BLOCKS_PER_STEP + r
        case = jnp.where(i == 0, 0, jnp.where(i == nblk - 1, 2, 1))
        start_row = jnp.clip(i * NA_Q_ROWS - NA_WIN_ROWS // 2, 0, rows - NA_K_ROWS)
        start = pl.multiple_of(start_row * GRID_W, GRID_W)
        rs = slice(r * tq, (r + 1) * tq)
        q = q_ref[rs, :] * (HEAD_DIM ** -0.5)
        kw = k_ref[pl.ds(start, nk), :]
        vw = v_ref[pl.ds(start, nk), :]
        for h in range(NA_HEADS):
            sl = slice(h * HEAD_DIM, (h + 1) * HEAD_DIM)
            s = lax.dot_general(q[:, sl], kw[:, sl], (((1,), (1,)), ((), ())), preferred_element_type=F32)
            s = s + bias_ref[case, h]
            m = jnp.max(s, axis=-1, keepdims=True)
            p = jnp.exp(s - m)
            l = jnp.sum(p, axis=-1, keepdims=True)
            o = jnp.dot(p.astype(BF16), vw[:, sl], preferred_element_type=F32)
            o_ref[rs, sl] = (o / l).astype(o_ref.dtype)


def _na(qkv, bias, bsz, t):
    rows = t // GRID_W
    tq = NA_BLOCKS_PER_STEP * NA_Q_ROWS * GRID_W
    qkv3 = qkv.reshape(bsz, t, 3 * NA_DIM)
    out = pl.pallas_call(
        functools.partial(_na_kernel, rows=rows),
        grid=(bsz, t // tq),
        in_specs=[pl.BlockSpec((None, tq, NA_DIM), lambda b, i: (b, i, 0)),
                  pl.BlockSpec((None, t, NA_DIM), lambda b, i: (b, 0, 1)),
                  pl.BlockSpec((None, t, NA_DIM), lambda b, i: (b, 0, 2)),
                  pl.BlockSpec(bias.shape, lambda b, i: (0, 0, 0, 0), pipeline_mode=pl.Buffered(1))],
        out_specs=pl.BlockSpec((None, tq, NA_DIM), lambda b, i: (b, i, 0)),
        out_shape=jax.ShapeDtypeStruct((bsz, t, NA_DIM), BF16),
        compiler_params=_params("parallel", "arbitrary"), name="na_attn",
    )(qkv3, qkv3, qkv3, bias)
    return out.reshape(bsz * t, NA_DIM)


def _tri(n, lower):
    r = lax.broadcasted_iota(jnp.int32, (n, n), 0)
    c = lax.broadcasted_iota(jnp.int32, (n, n), 1)
    return (r >= c) if lower else (r <= c)


def _split3(x):
    hi = x.astype(BF16).astype(F32)
    r = x - hi
    mid = r.astype(BF16).astype(F32)
    lo = (r - mid).astype(BF16).astype(F32)
    return hi, mid, lo


def _dot01_left(m01, x):
    return sum(jnp.dot(m01, p.astype(BF16), preferred_element_type=F32) for p in _split3(x))


def _dot01_right(x, m01):
    return sum(jnp.dot(p.astype(BF16), m01, preferred_element_type=F32) for p in _split3(x))


def _head_expand():
    r = lax.broadcasted_iota(jnp.int32, (SSD_HEADS, SSD_DIM), 0)
    c = lax.broadcasted_iota(jnp.int32, (SSD_HEADS, SSD_DIM), 1)
    return (c // HEAD_DIM == r).astype(BF16)


def _ssd_state_kernel(xf_ref, xb_ref, btf_ref, btb_ref, dttf_ref, dttb_ref, alog_ref,
                      pf_ref, pb_ref, sf_ref, sb_ref):
    q = SSD_CHUNK
    nh = SSD_HEADS
    pair_w = 2 * HEAD_DIM

    @pl.when(pl.program_id(1) == 0)
    def _():
        sf_ref[...] = jnp.zeros_like(sf_ref)
        sb_ref[...] = jnp.zeros_like(sb_ref)

    row_id = lax.broadcasted_iota(jnp.int32, (q, q), 0)
    col_id = lax.broadcasted_iota(jnp.int32, (q, q), 1)
    first_half = lax.broadcasted_iota(jnp.int32, (SSD_STATE, pair_w), 1) < HEAD_DIM
    a_col = -jnp.exp(alog_ref[...])
    for d, (x_ref, bt_ref, dtt_ref, prev_ref, state_ref) in enumerate(
            ((xf_ref, btf_ref, dttf_ref, pf_ref, sf_ref), (xb_ref, btb_ref, dttb_ref, pb_ref, sb_ref))):
        backward = d == 1
        hs = slice(d * nh, (d + 1) * nh)
        tri = ((row_id >= col_id) if backward else (row_id <= col_id)).astype(BF16)
        dtr = dtt_ref[hs, :]
        da = dtr * a_col[hs]
        local, decay = [], []
        da_rows = jnp.concatenate([da[:, cc * q:(cc + 1) * q] for cc in range(SSD_BLOCK_CHUNKS)], axis=0)
        cs_rows = _dot01_right(da_rows, tri)
        for cc in range(SSD_BLOCK_CHUNKS):
            rs = slice(cc * q, (cc + 1) * q)
            cs = cs_rows[cc * nh:(cc + 1) * nh]
            tot = cs[:, 0:1] if backward else cs[:, q - 1:q]
            w = jnp.exp(tot - cs) * dtr[:, rs]
            chunk_decay = jnp.exp(tot)
            local.append([])
            decay.append([])
            for pp in range(nh // 2):
                g = (2 * pp) // (nh // SSD_GROUPS)
                bt = bt_ref[g * SSD_STATE:(g + 1) * SSD_STATE, rs].astype(F32)
                xs = x_ref[rs, pp * pair_w:(pp + 1) * pair_w]
                lhs = jnp.concatenate([(bt * w[h:h + 1, :]).astype(BF16) for h in (2 * pp, 2 * pp + 1)], axis=0)
                both = jnp.dot(lhs, xs, preferred_element_type=F32)
                local[cc].append(jnp.where(first_half, both[:SSD_STATE], both[SSD_STATE:]))
                decay[cc].append(jnp.where(first_half, chunk_decay[2 * pp:2 * pp + 1, :],
                                           chunk_decay[2 * pp + 1:2 * pp + 2, :]))
        order = range(SSD_BLOCK_CHUNKS - 1, -1, -1) if backward else range(SSD_BLOCK_CHUNKS)
        for pp in range(nh // 2):
            lanes = slice(pp * pair_w, (pp + 1) * pair_w)
            state = state_ref[:, lanes]
            for cc in order:
                prev_ref[cc, :, lanes] = state.astype(prev_ref.dtype)
                state = state * decay[cc][pp] + local[cc][pp]
            state_ref[:, lanes] = state


def _ssd_states(xc3, bt, dtt, a_log16):
    bsz, t, _ = xc3.shape
    tb = SSD_BLOCK_CHUNKS * SSD_CHUNK
    nblk = t // tb
    bn = SSD_GROUPS * SSD_STATE
    fwd_rows = pl.BlockSpec((None, tb, SSD_CONV_DIM), lambda b, j: (b, j, 0))
    bwd_rows = pl.BlockSpec((None, tb, SSD_CONV_DIM), lambda b, j: (b, nblk - 1 - j, 0))
    fwd_cols = lambda rows: pl.BlockSpec((rows, tb), lambda b, j: (0, b * nblk + j))
    bwd_cols = lambda rows: pl.BlockSpec((rows, tb), lambda b, j: (0, b * nblk + nblk - 1 - j))
    state = jax.ShapeDtypeStruct((bsz, t // SSD_CHUNK, SSD_STATE, SSD_DIM), BF16)
    return pl.pallas_call(
        _ssd_state_kernel, grid=(bsz, nblk),
        in_specs=[fwd_rows, bwd_rows, fwd_cols(bn), bwd_cols(bn), fwd_cols(2 * SSD_HEADS), bwd_cols(2 * SSD_HEADS),
                  _const_spec((2 * SSD_HEADS, 1))],
        out_specs=[pl.BlockSpec((None, SSD_BLOCK_CHUNKS, SSD_STATE, SSD_DIM), lambda b, j: (b, j, 0, 0)),
                   pl.BlockSpec((None, SSD_BLOCK_CHUNKS, SSD_STATE, SSD_DIM),
                                lambda b, j: (b, nblk - 1 - j, 0, 0))],
        out_shape=[state, state],
        scratch_shapes=[pltpu.VMEM((SSD_STATE, SSD_DIM), F32), pltpu.VMEM((SSD_STATE, SSD_DIM), F32)],
        compiler_params=_params("parallel", "arbitrary"), name="ssd_state",
    )(xc3, xc3, bt, bt, dtt, dtt, a_log16.reshape(2 * SSD_HEADS, 1))


def _ssd_out_kernel(x_ref, z_ref, dtt_ref, pf_ref, pb_ref, alog_col_ref, dskip_ref, nw_ref, o_ref):
    q = SSD_CHUNK
    nh = SSD_HEADS
    row_id = lax.broadcasted_iota(jnp.int32, (q, q), 0)
    col_id = lax.broadcasted_iota(jnp.int32, (q, q), 1)
    lower = row_id >= col_id
    eye = row_id == col_id
    first_half = lax.broadcasted_iota(jnp.int32, (q, 2 * HEAD_DIM), 1) < HEAD_DIM
    tri_l = lower.astype(BF16)
    tri_u = (row_id <= col_id).astype(BF16)
    a_col = -jnp.exp(alog_col_ref[...])
    dtt_all = dtt_ref[...]
    da = dtt_all * a_col
    da_rows = jnp.concatenate([da[:, cc * q:(cc + 1) * q] for cc in range(SSD_BLOCK_CHUNKS)], axis=0)
    cs_rows_f, cs_rows_b = _dot01_right(da_rows, tri_u), _dot01_right(da_rows, tri_l)
    cs_cols_f, cs_cols_b = cs_rows_f.T, cs_rows_b.T
    e_cols_f, e_cols_b = jnp.exp(cs_cols_f), jnp.exp(cs_cols_b)
    for cc in range(SSD_BLOCK_CHUNKS):
        rs = slice(cc * q, (cc + 1) * q)
        dtr = dtt_all[:, rs]
        hrow = slice(cc * 2 * nh, (cc + 1) * 2 * nh)
        cs_row_f, cs_row_b = cs_rows_f[hrow], cs_rows_b[hrow]
        cs_col_f, cs_col_b = cs_cols_f[:, hrow], cs_cols_b[:, hrow]
        e_col_f, e_col_b = e_cols_f[:, hrow], e_cols_b[:, hrow]
        bc = x_ref[rs, SSD_DIM:SSD_CONV_DIM]
        groups = []
        for g in range(SSD_GROUPS):
            bg = bc[:, g * SSD_STATE:(g + 1) * SSD_STATE]
            cg = bc[:, (SSD_GROUPS + g) * SSD_STATE:(SSD_GROUPS + g + 1) * SSD_STATE]
            cb = lax.dot_general(cg, bg, (((1,), (1,)), ((), ())), preferred_element_type=F32)
            c2 = jnp.concatenate([cg, cg], axis=1).astype(F32)
            pairs = []
            for pp in range(SSD_HEADS // SSD_GROUPS // 2):
                h0 = g * (SSD_HEADS // SSD_GROUPS) + 2 * pp
                lanes = slice(h0 * HEAD_DIM, (h0 + 2) * HEAD_DIM)
                rhs = jnp.concatenate([x_ref[rs, lanes], pf_ref[cc, :, lanes], pb_ref[cc, :, lanes]], axis=0)
                lhs = []
                for h in (h0, h0 + 1):
                    seg_f = cs_col_f[:, h:h + 1] - cs_row_f[h:h + 1, :]
                    seg_b = cs_col_b[:, nh + h:nh + h + 1] - cs_row_b[nh + h:nh + h + 1, :]
                    dt_f, dt_b = dtr[h:h + 1, :], dtr[nh + h:nh + h + 1, :]
                    lmat = (jnp.exp(jnp.where(lower, seg_f, seg_b)) * jnp.where(lower, dt_f, dt_b)
                            + jnp.where(eye, dt_b, 0.0))
                    e2 = jnp.where(first_half, e_col_f[:, h:h + 1], e_col_b[:, nh + h:nh + h + 1])
                    lhs.append(jnp.concatenate([(cb * lmat).astype(BF16), (c2 * e2).astype(BF16)], axis=1))
                both = jnp.dot(jnp.concatenate(lhs, axis=0), rhs, preferred_element_type=F32)
                pairs.append(jnp.where(first_half, both[:q], both[q:]))
            groups.append(jnp.concatenate(pairs, axis=1))
        y = jnp.concatenate(groups, axis=1) + dskip_ref[...] * x_ref[rs, :SSD_DIM].astype(F32)
        yg = y * _silu(z_ref[rs, :])
        ms = jnp.mean(yg * yg, axis=-1, keepdims=True)
        o_ref[rs, :] = (yg * lax.rsqrt(ms + LN_EPS) * nw_ref[...]).astype(o_ref.dtype)


def _ssd_out(xc3, z3, dtt, prev_f, prev_b, a_log16, d_skip, norm_w):
    bsz, t, _ = xc3.shape
    tb = SSD_BLOCK_CHUNKS * SSD_CHUNK
    nblk = t // tb
    blk = lambda wd: pl.BlockSpec((None, tb, wd), lambda b, j: (b, j, 0))
    prev = pl.BlockSpec((None, SSD_BLOCK_CHUNKS, SSD_STATE, SSD_DIM), lambda b, j: (b, j, 0, 0))
    out = pl.pallas_call(
        _ssd_out_kernel, grid=(bsz, nblk),
        in_specs=[blk(SSD_CONV_DIM), blk(SSD_DIM),
                  pl.BlockSpec((2 * SSD_HEADS, tb), lambda b, j: (0, b * nblk + j)),
                  prev, prev, _const_spec((2 * SSD_HEADS, 1)),
                  _const_spec((1, SSD_DIM)), _const_spec((1, SSD_DIM))],
        out_specs=blk(SSD_DIM),
        out_shape=jax.ShapeDtypeStruct((bsz, t, SSD_DIM), BF16),
        compiler_params=_params("parallel", "parallel"), name="ssd_out",
    )(xc3, z3, dtt, prev_f, prev_b, a_log16.reshape(2 * SSD_HEADS, 1), d_skip, norm_w)
    return out.reshape(bsz * t, SSD_DIM)


def _s5_tables(lam_re, lam_im, log_dt, b_re, b_im, c_re, c_im):
    lc = S5_CHUNK
    hp = HIGHEST
    pw_re, pw_im, bb_re, bb_im = [], [], [], []
    for d in range(2):
        dt = jnp.exp(log_dt[d])[:, None]
        lr, li = lam_re[d], lam_im[d]
        mag = jnp.exp(lr * dt)
        ar, ai = mag * jnp.cos(li * dt), mag * jnp.sin(li * dt)
        den = lr * lr + li * li
        fr = ((ar - 1.0) * lr + ai * li) / den
        fi = (ai * lr - (ar - 1.0) * li) / den
        bb_re.append(fr[..., None] * b_re[d] - fi[..., None] * b_im[d])
        bb_im.append(fr[..., None] * b_im[d] + fi[..., None] * b_re[d])
        pr, pi = [jnp.ones_like(ar)], [jnp.zeros_like(ar)]
        for _ in range(lc):
            pr, pi = pr + [pr[-1] * ar - pi[-1] * ai], pi + [pr[-1] * ai + pi[-1] * ar]
        pw_re.append(jnp.stack(pr))
        pw_im.append(jnp.stack(pi))

    def kern(d):
        wr = pw_re[d][:lc, :, :, None] * bb_re[d][None] - pw_im[d][:lc, :, :, None] * bb_im[d][None]
        wi = pw_re[d][:lc, :, :, None] * bb_im[d][None] + pw_im[d][:lc, :, :, None] * bb_re[d][None]
        return (jnp.einsum('ghp,kgpj->kghj', c_re[d], wr, precision=hp)
                - jnp.einsum('ghp,kgpj->kghj', c_im[d], wi, precision=hp))

    kf, kb = kern(0), kern(1)
    l = jnp.arange(lc)
    lag = l[:, None] - l[None, :]
    tf = jnp.where((lag >= 0)[:, :, None, None, None], kf[jnp.clip(lag, 0, lc - 1)], 0.0)
    tb = jnp.where((lag <= 0)[:, :, None, None, None], kb[jnp.clip(-lag, 0, lc - 1)], 0.0)
    toep = (tf + tb).transpose(2, 1, 4, 0, 3).reshape(S5_GROUPS, lc * S5_GROUP_CH, lc * S5_GROUP_CH)

    def state_in(d, powers):
        wr = pw_re[d][powers][:, :, :, None] * bb_re[d][None] - pw_im[d][powers][:, :, :, None] * bb_im[d][None]
        wi = pw_re[d][powers][:, :, :, None] * bb_im[d][None] + pw_im[d][powers][:, :, :, None] * bb_re[d][None]
        to_rows = lambda w: w.transpose(1, 0, 3, 2).reshape(S5_GROUPS, lc * S5_GROUP_CH, S5_STATE)
        return to_rows(wr), to_rows(wi)

    f_re, f_im = state_in(0, lc - 1 - l)
    b_re_, b_im_ = state_in(1, l)
    m_state = jnp.concatenate([f_re, b_re_, f_im, b_im_], axis=-1)

    def state_out(d, powers):
        cpr = c_re[d][None] * pw_re[d][powers][:, :, None, :] - c_im[d][None] * pw_im[d][powers][:, :, None, :]
        cpi = c_re[d][None] * pw_im[d][powers][:, :, None, :] + c_im[d][None] * pw_re[d][powers][:, :, None, :]
        to_cols = lambda w: w.transpose(1, 3, 0, 2).reshape(S5_GROUPS, S5_STATE, lc * S5_GROUP_CH)
        return to_cols(cpr), -to_cols(cpi)

    of_re, of_im = state_out(0, l + 1)
    ob_re, ob_im = state_out(1, lc - l)
    m_off = jnp.concatenate([of_re, ob_re, of_im, ob_im], axis=1)
    dec_re = jnp.concatenate([pw_re[0][lc], pw_re[1][lc]], axis=-1)[:, None, :]
    dec_im = jnp.concatenate([pw_im[0][lc], pw_im[1][lc]], axis=-1)[:, None, :]
    return toep.astype(BF16), m_state.astype(BF16), m_off.astype(BF16), dec_re, dec_im


S5_W = S5_CHUNK * S5_GROUP_CH
S5_RELAYOUT_CHUNKS = 64
S5_GROUPS_PER_STEP = 2


def _s5_group_kernel(u_ref, o_ref):
    x = u_ref[...]
    for g in range(S5_GROUPS):
        lo = g * S5_GROUP_CH
        pieces = [x[:, s * S5_DIM + lo:s * S5_DIM + lo + S5_GROUP_CH] for s in range(S5_CHUNK)]
        o_ref[g] = jnp.concatenate(pieces, axis=-1).astype(o_ref.dtype)


def _s5_ungroup_kernel(y_ref, u_ref, d_ref, o_ref):
    ys = [y_ref[g] for g in range(S5_GROUPS)]
    for l in range(S5_CHUNK):
        cols = slice(l * S5_DIM, (l + 1) * S5_DIM)
        pieces = [y[:, l * S5_GROUP_CH:(l + 1) * S5_GROUP_CH] for y in ys]
        o_ref[:, cols] = jnp.concatenate(pieces, axis=-1) + d_ref[...] * u_ref[:, cols]


def _s5_kernel(u_ref, toep_ref, mst_ref, moff_ref, are_ref, aim_ref, y_ref, s_ref, e_ref, *, nc, bsz):
    ng = S5_GROUPS_PER_STEP
    half = S5_STATE
    for gi in range(ng):
        ub = u_ref[gi]
        y_ref[gi] = jnp.dot(ub, toep_ref[gi], preferred_element_type=F32)
        s_ref[gi] = jnp.dot(ub, mst_ref[gi], preferred_element_type=F32)
    ar = [jnp.broadcast_to(are_ref[gi], (bsz, 2 * half)) for gi in range(ng)]
    ai = [jnp.broadcast_to(aim_ref[gi], (bsz, 2 * half)) for gi in range(ng)]
    is_fwd = lax.broadcasted_iota(jnp.int32, (bsz, 2 * half), 1) < half

    def body(i, carry):
        rf = pl.multiple_of(i * bsz, bsz)
        rb = pl.multiple_of((nc - 1 - i) * bsz, bsz)
        out = []
        for gi in range(ng):
            er, ei = carry[gi]
            e_ref[gi, pl.ds(rf, bsz), 0:half] = er[:, :half]
            e_ref[gi, pl.ds(rb, bsz), half:2 * half] = er[:, half:]
            e_ref[gi, pl.ds(rf, bsz), 2 * half:3 * half] = ei[:, :half]
            e_ref[gi, pl.ds(rb, bsz), 3 * half:4 * half] = ei[:, half:]
            sf = s_ref[gi, pl.ds(rf, bsz), :]
            sb = s_ref[gi, pl.ds(rb, bsz), :]
            sr = jnp.where(is_fwd, sf[:, :2 * half], sb[:, :2 * half])
            si = jnp.where(is_fwd, sf[:, 2 * half:], sb[:, 2 * half:])
            out.append((ar[gi] * er - ai[gi] * ei + sr, ar[gi] * ei + ai[gi] * er + si))
        return tuple(out)

    zero = jnp.zeros((bsz, 2 * half), F32)
    lax.fori_loop(0, nc, body, tuple((zero, zero) for _ in range(ng)), unroll=2)
    for gi in range(ng):
        y_ref[gi] += jnp.dot(e_ref[gi].astype(BF16), moff_ref[gi], preferred_element_type=F32)


def _s5(u_cm, tables, d_skip, bsz, t):
    toep, m_state, m_off, dec_re, dec_im = tables
    nc = t // S5_CHUNK
    w = S5_W
    rc = min(S5_RELAYOUT_CHUNKS, nc)
    u3 = u_cm.reshape(bsz, nc, S5_CHUNK * S5_DIM)
    tok = pl.BlockSpec((None, rc, S5_CHUNK * S5_DIM), lambda b, j: (b, j, 0))
    grouped = pl.BlockSpec((S5_GROUPS, rc, w), lambda b, j: (0, j, b))
    ug = pl.pallas_call(
        _s5_group_kernel, grid=(bsz, nc // rc), in_specs=[tok], out_specs=grouped,
        out_shape=jax.ShapeDtypeStruct((S5_GROUPS, nc, bsz * w), BF16),
        compiler_params=_params("parallel", "parallel"), name="s5_group",
    )(u3)
    ng = S5_GROUPS_PER_STEP
    grp = lambda shape: pl.BlockSpec((ng,) + shape, lambda g: (g, 0, 0))
    yg = pl.pallas_call(
        functools.partial(_s5_kernel, nc=nc, bsz=bsz),
        grid=(S5_GROUPS // ng,),
        in_specs=[grp((nc * bsz, w)), grp((w, w)), grp((w, w)), grp((w, w)),
                  grp((1, 2 * S5_STATE)), grp((1, 2 * S5_STATE))],
        out_specs=grp((nc * bsz, w)),
        out_shape=jax.ShapeDtypeStruct((S5_GROUPS, nc * bsz, w), F32),
        scratch_shapes=[pltpu.VMEM((ng, nc * bsz, w), F32), pltpu.VMEM((ng, nc * bsz, w), F32)],
        compiler_params=_params("parallel"), name="s5_scan",
    )(ug.reshape(S5_GROUPS, nc * bsz, w), toep, m_state, m_off, dec_re, dec_im)
    y = pl.pallas_call(
        _s5_ungroup_kernel, grid=(bsz, nc // rc),
        in_specs=[grouped, tok, pl.BlockSpec((1, S5_DIM), lambda b, j: (0, 0))], out_specs=tok,
        out_shape=jax.ShapeDtypeStruct((bsz, nc, S5_CHUNK * S5_DIM), F32),
        compiler_params=_params("parallel", "parallel"), name="s5_ungroup",
    )(yg.reshape(S5_GROUPS, nc, bsz * w), u3, d_skip)
    return y.reshape(bsz * nc, S5_CHUNK * S5_DIM)


def _mixout_kernel(h_ref, na_ref, ssd_ref, y5_ref, gw_ref, gb_ref, wo_ref, g_ref, b_ref, o_ref, ys_ref):
    tm = h_ref.shape[0]
    n_slab = S5_DIM // LANES
    for l in range(S5_CHUNK):
        for k in range(n_slab):
            ys_ref[k, pl.ds(l, tm // S5_CHUNK, stride=S5_CHUNK), :] = \
                y5_ref[:, l * S5_DIM + k * LANES:l * S5_DIM + (k + 1) * LANES]
    half = tm // 2
    for r in range(2):
        rs = slice(r * half, (r + 1) * half)
        y5 = jnp.concatenate([ys_ref[k, rs, :] for k in range(n_slab)], axis=-1)
        gl = 0.5 * y5 * (1.0 + jnp.tanh(math.sqrt(2.0 / math.pi) * (y5 + 0.044715 * (y5 * y5 * y5))))
        gate = jnp.dot(gl.astype(BF16), gw_ref[...], preferred_element_type=F32) + gb_ref[...]
        o5 = gl * _sigmoid(gate)
        mixed = jnp.concatenate([na_ref[rs, :], ssd_ref[rs, :], o5.astype(BF16)], axis=-1)
        mix = jnp.dot(mixed, wo_ref[...], preferred_element_type=F32)
        o_ref[rs, :] = _layernorm(DEEPNORM_ALPHA * h_ref[rs, :] + mix, g_ref[...], b_ref[...])


def _mixout(h, o_na, o_ssd, y5_cm, glu_w, glu_b, w_out, g, b):
    n = h.shape[0]
    tm = MIX_ROW_TILE
    row = lambda wd: pl.BlockSpec((tm, wd), lambda i: (i, 0))
    return pl.pallas_call(
        _mixout_kernel, grid=(n // tm,),
        in_specs=[row(D_MODEL), row(NA_DIM), row(SSD_DIM),
                  pl.BlockSpec((tm // S5_CHUNK, S5_CHUNK * S5_DIM), lambda i: (i, 0)),
                  _const_spec((S5_DIM, S5_DIM)), _const_spec((1, S5_DIM)),
                  _const_spec((D_MODEL, D_MODEL)), _const_spec((1, D_MODEL)), _const_spec((1, D_MODEL))],
        out_specs=row(D_MODEL), out_shape=jax.ShapeDtypeStruct((n, D_MODEL), F32),
        scratch_shapes=[pltpu.VMEM((S5_DIM // LANES, tm, LANES), F32)],
        compiler_params=_params("parallel"), name="mix_out",
    )(h, o_na, o_ssd, y5_cm, glu_w, glu_b, w_out, g, b)


def _kvproj_kernel(m_ref, wk_ref, wv_ref, k_ref, v_ref):
    mb = m_ref[...].astype(BF16)
    k_ref[...] = jnp.dot(mb, wk_ref[...], preferred_element_type=F32).astype(k_ref.dtype)
    v_ref[...] = jnp.dot(mb, wv_ref[...], preferred_element_type=F32).astype(v_ref.dtype)


def _kvproj(mem2, wk, wv):
    n = mem2.shape[0]
    tm = min(ROW_TILE, n)
    row = pl.BlockSpec((tm, D_MODEL), lambda i: (i, 0))
    return pl.pallas_call(
        _kvproj_kernel, grid=(n // tm,),
        in_specs=[row, _const_spec((D_MODEL, D_MODEL)), _const_spec((D_MODEL, D_MODEL))],
        out_specs=[row, row], out_shape=[jax.ShapeDtypeStruct((n, D_MODEL), BF16)] * 2,
        compiler_params=_params("parallel"), name="xa_kv_proj",
    )(mem2, wk, wv)


def _xattn_kernel(h_ref, k_ref, v_ref, wq_ref, wo_ref, g_ref, b_ref, o_ref, oh_ref):
    half = h_ref.shape[0] // 2
    for r in range(2):
        rs = slice(r * half, (r + 1) * half)
        h = h_ref[rs, :]
        q = jnp.dot(h.astype(BF16), wq_ref[...], preferred_element_type=F32) * (XA_HEAD_DIM ** -0.5)
        qb = q.astype(BF16)
        for hd in range(XA_HEADS):
            sl = slice(hd * XA_HEAD_DIM, (hd + 1) * XA_HEAD_DIM)
            s = lax.dot_general(qb[:, sl], k_ref[:, sl], (((1,), (1,)), ((), ())), preferred_element_type=F32)
            m = jnp.max(s, axis=-1, keepdims=True)
            p = jnp.exp(s - m)
            l = jnp.sum(p, axis=-1, keepdims=True)
            o = jnp.dot(p.astype(BF16), v_ref[:, sl], preferred_element_type=F32) / l
            oh_ref[rs, sl] = o.astype(oh_ref.dtype)
        xa = jnp.dot(oh_ref[rs, :], wo_ref[...], preferred_element_type=F32)
        o_ref[rs, :] = _layernorm(DEEPNORM_ALPHA * h + xa, g_ref[...], b_ref[...])


def _xattn(h, k3, v3, wq, wo, g, b, bsz, t):
    m = k3.shape[1]
    h3 = h.reshape(bsz, t, D_MODEL)
    tm = XA_ROW_TILE
    row = pl.BlockSpec((None, tm, D_MODEL), lambda bb, i: (bb, i, 0))
    kv = pl.BlockSpec((None, m, D_MODEL), lambda bb, i: (bb, 0, 0))
    const = lambda shape: pl.BlockSpec(shape, lambda bb, i: (0, 0))
    out = pl.pallas_call(
        _xattn_kernel, grid=(bsz, t // tm),
        in_specs=[row, kv, kv, const((D_MODEL, D_MODEL)), const((D_MODEL, D_MODEL)),
                  const((1, D_MODEL)), const((1, D_MODEL))],
        out_specs=row, out_shape=jax.ShapeDtypeStruct((bsz, t, D_MODEL), F32),
        scratch_shapes=[pltpu.VMEM((tm, D_MODEL), BF16)],
        compiler_params=_params("parallel", "parallel"), name="cross_attn",
    )(h3, k3, v3, wq, wo, g, b)
    return out.reshape(bsz * t, D_MODEL)


FF_TILE = 1024


def _mlp_kernel(h_ref, w1_ref, w2_ref, g_ref, b_ref, o_ref, a_ref):
    half = h_ref.shape[0] // 2
    for r in range(2):
        rs = slice(r * half, (r + 1) * half)
        h = h_ref[rs, :]
        hb = h.astype(BF16)
        for c in range(D_FF // FF_TILE):
            sl = slice(c * FF_TILE, (c + 1) * FF_TILE)
            a = jnp.maximum(jnp.dot(hb, w1_ref[:, sl], preferred_element_type=F32), 0.0)
            a_ref[rs, sl] = (a * a).astype(a_ref.dtype)
        ff = jnp.dot(a_ref[rs, :], w2_ref[...], preferred_element_type=F32)
        o_ref[rs, :] = _layernorm(DEEPNORM_ALPHA * h + ff, g_ref[...], b_ref[...])


def _mlp(h, w1, w2, g, b):
    n = h.shape[0]
    row = pl.BlockSpec((MLP_ROW_TILE, D_MODEL), lambda i: (i, 0))
    return pl.pallas_call(
        _mlp_kernel, grid=(n // MLP_ROW_TILE,),
        in_specs=[row,
                  pl.BlockSpec((D_MODEL, D_FF), lambda i: (0, 0), pipeline_mode=pl.Buffered(1)),
                  pl.BlockSpec((D_FF, D_MODEL), lambda i: (0, 0), pipeline_mode=pl.Buffered(1)),
                  _const_spec((1, D_MODEL)), _const_spec((1, D_MODEL))],
        out_specs=row, out_shape=jax.ShapeDtypeStruct((n, D_MODEL), F32),
        scratch_shapes=[pltpu.VMEM((MLP_ROW_TILE, D_FF), BF16)],
        compiler_params=_params("parallel"), name="mlp",
    )(h, w1, w2, g, b)


def _permute_w_in(w):
    dt_lo = 3 * NA_DIM + SSD_DIM + SSD_CONV_DIM
    dt_hi = dt_lo + 2 * SSD_HEADS
    pad = jnp.zeros((w.shape[0], DT_PAD - 2 * SSD_HEADS), w.dtype)
    return jnp.concatenate([w[:, :dt_lo], w[:, dt_hi:], w[:, dt_lo:dt_hi], pad], axis=1).astype(BF16)


def kernel(x, mem, ln_in_g, ln_in_b, w_in, na_rpb, ssd_conv_w, ssd_conv_b, ssd_dt_bias, ssd_a_log, ssd_d,
           ssd_norm_w, s5_lam_re, s5_lam_im, s5_log_dt, s5_b_re, s5_b_im, s5_c_re, s5_c_im, s5_d, s5_glu_w,
           s5_glu_b, w_mix_out, ln_mix_g, ln_mix_b, xa_wq, xa_wk, xa_wv, xa_wo, ln_xa_g, ln_xa_b, mlp_w1,
           mlp_w2, ln_mlp_g, ln_mlp_b):
    bsz, t, _ = x.shape
    n = bsz * t
    rows = t // GRID_W
    assert all(t % tile == 0 for tile in (IN_ROW_TILE, MIX_ROW_TILE, XA_ROW_TILE, MLP_ROW_TILE))
    assert t % (SSD_BLOCK_CHUNKS * SSD_CHUNK) == 0 and rows % NA_Q_ROWS == 0
    assert rows >= NA_K_ROWS and t % S5_CHUNK == 0
    row1 = lambda v: v.reshape(1, -1).astype(F32)
    mem2 = mem.reshape(-1, D_MODEL)
    h = x.reshape(n, D_MODEL)
    for l in range(DEPTH):
        dt_bias = jnp.pad(row1(ssd_dt_bias[l]), ((0, 0), (0, DT_PAD - 2 * SSD_HEADS)))
        outs = _inproj(h, row1(ln_in_g), row1(ln_in_b), _permute_w_in(w_in[l]), ssd_conv_w[l].astype(F32),
                       row1(ssd_conv_b[l]), dt_bias, t, apply_ln=(l == 0))
        qkv, z, xc, bt, u_cm, dtt = outs[:6]
        if l == 0:
            h = outs[6]
        o_na = _na(qkv, _na_bias_tables(na_rpb[l].astype(F32), rows), bsz, t)

        xc3 = xc.reshape(bsz, t, SSD_CONV_DIM)
        a_log16 = row1(ssd_a_log[l])
        prev_f, prev_b = _ssd_states(xc3, bt, dtt, a_log16)
        o_ssd = _ssd_out(xc3, z.reshape(bsz, t, SSD_DIM), dtt, prev_f, prev_b, a_log16,
                         row1(jnp.repeat(ssd_d[l], HEAD_DIM)), row1(ssd_norm_w[l]))

        tables = _s5_tables(s5_lam_re[l].astype(F32), s5_lam_im[l].astype(F32), s5_log_dt[l].astype(F32),
                            s5_b_re[l].astype(F32), s5_b_im[l].astype(F32), s5_c_re[l].astype(F32),
                            s5_c_im[l].astype(F32))
        y5 = _s5(u_cm, tables, row1(s5_d[l]), bsz, t)

        h = _mixout(h, o_na, o_ssd, y5, s5_glu_w[l].astype(BF16), row1(s5_glu_b[l]),
                    w_mix_out[l].astype(BF16), row1(ln_mix_g[l]), row1(ln_mix_b[l]))
        k2, v2 = _kvproj(mem2, xa_wk[l].astype(BF16), xa_wv[l].astype(BF16))
        h = _xattn(h, k2.reshape(bsz, -1, D_MODEL), v2.reshape(bsz, -1, D_MODEL), xa_wq[l].astype(BF16),
                   xa_wo[l].astype(BF16), row1(ln_xa_g[l]), row1(ln_xa_b[l]), bsz, t)
        h = _mlp(h, mlp_w1[l].astype(BF16), mlp_w2[l].astype(BF16), row1(ln_mlp_g[l]), row1(ln_mlp_b[l]))
    return h.reshape(bsz, t, D_MODEL)
```

```python
import functools
import math

import jax
import jax.numpy as jnp
import numpy as np
from jax import lax
from jax.experimental import pallas as pl
from jax.experimental.pallas import tpu as pltpu

F32 = jnp.float32
BF16 = jnp.bfloat16
HIGHEST = lax.Precision.HIGHEST

D_MODEL = 1024
DEPTH = 2
GRID_W = 64
HEAD_DIM = 64
NA_DIM = 256
NA_HEADS = 4
NA_WIN_ROWS = 8
NA_WIN_COLS = 16
SSD_DIM = 512
SSD_HEADS = 8
SSD_GROUPS = 2
SSD_STATE = 64
SSD_CONV = 5
SSD_CHUNK = 128
SSD_CONV_DIM = SSD_DIM + 2 * SSD_GROUPS * SSD_STATE
S5_DIM = 256
S5_GROUP_CH = 16
S5_GROUPS = 16
S5_STATE = 64
XA_HEADS = 4
XA_HEAD_DIM = 256
D_FF = 4096
LN_EPS = 1e-5
NEG_BIG = -1e30
DEEPNORM_ALPHA = (2 * DEPTH) ** 0.25

ROW_TILE = 512
XA_ROW_TILE = 1024
MIX_ROW_TILE = 1024
MLP_ROW_TILE = 1024
IN_ROW_TILE = 1024
NA_Q_ROWS = 4
NA_K_ROWS = NA_Q_ROWS + NA_WIN_ROWS
NA_BLOCKS_PER_STEP = 4
SSD_BLOCK_CHUNKS = 8
S5_CHUNK = 16
LANES = 128
DT_PAD = LANES
VMEM_LIMIT = 56 * 1024 * 1024


def _params(*sem):
    return pltpu.CompilerParams(dimension_semantics=sem, vmem_limit_bytes=VMEM_LIMIT)


def _layernorm(x, g, b):
    mu = jnp.mean(x, axis=-1, keepdims=True)
    xc = x - mu
    var = jnp.mean(xc * xc, axis=-1, keepdims=True)
    return xc * lax.rsqrt(var + LN_EPS) * g + b


def _sigmoid(x):
    return 0.5 + 0.5 * jnp.tanh(0.5 * x)


def _silu(x):
    return x * _sigmoid(x)


def _softplus(x):
    return jnp.maximum(x, 0.0) + jnp.log1p(jnp.exp(-jnp.abs(x)))


def _const_spec(shape):
    n = len(shape)
    return pl.BlockSpec(shape, lambda *_: (0,) * n)


IN_QKV, IN_Z, IN_XBC, IN_U, IN_DT = (0, 768), (768, 1280), (1280, 2048), (2048, 2304), (2304, 2304 + DT_PAD)
PROJ_HALO = 16


def _inproj_kernel(x_ref, xp_ref, xn_ref, g_ref, b_ref, w_ref, cw_ref, cb_ref, dtb_ref,
                   qkv_ref, z_ref, xc_ref, bt_ref, u_ref, dtt_ref, *rest, apply_ln, tiles_per_seq):
    xe_ref, ce_ref, us_ref = rest[-3:]
    tm, hl = IN_ROW_TILE, PROJ_HALO
    pos = pl.program_id(0) % tiles_per_seq
    x, xp, xn = x_ref[...], xp_ref[...], xn_ref[...]
    if apply_ln:
        x = _layernorm(x, g_ref[...], b_ref[...])
        xp = _layernorm(xp, g_ref[...], b_ref[...])
        xn = _layernorm(xn, g_ref[...], b_ref[...])
        rest[0][...] = x
    xe_ref[0:hl, :] = xp.astype(BF16)
    xe_ref[hl:hl + tm, :] = x.astype(BF16)
    xe_ref[hl + tm:, :] = xn.astype(BF16)

    half = tm // 2
    pad = SSD_CONV // 2
    n_slab = S5_DIM // LANES
    for r in range(2):
        rs = slice(r * half, (r + 1) * half)
        xbc = jnp.dot(xe_ref[r * half:r * half + half + 2 * hl, :], w_ref[:, IN_XBC[0]:IN_XBC[1]],
                      preferred_element_type=F32)
        before = jnp.where(pos == 0, 0.0, xbc[0:hl]) if r == 0 else xbc[0:hl]
        after = jnp.where(pos == tiles_per_seq - 1, 0.0, xbc[hl + half:]) if r == 1 else xbc[hl + half:]
        ce_ref[r, 0:hl, :] = before
        ce_ref[r, hl:hl + half, :] = xbc[hl:hl + half]
        ce_ref[r, hl + half:, :] = after

        xb = xe_ref[hl + r * half:hl + (r + 1) * half, :]

        def proj(cols):
            return jnp.dot(xb, w_ref[:, cols[0]:cols[1]], preferred_element_type=F32)

        qkv_ref[rs, :] = proj(IN_QKV).astype(qkv_ref.dtype)
        z_ref[rs, :] = proj(IN_Z)
        u = proj(IN_U)
        dt = _softplus(proj(IN_DT) + dtb_ref[...])
        dtt_ref[:, rs] = dt.T[:2 * SSD_HEADS, :]

        acc = cb_ref[...]
        for k in range(SSD_CONV):
            acc = acc + ce_ref[r, hl - pad + k:hl - pad + k + half, :] * cw_ref[k:k + 1, :]
        xc = _silu(acc)
        xc_ref[rs, :] = xc.astype(xc_ref.dtype)
        bt_ref[:, rs] = xc[:, SSD_DIM:SSD_DIM + SSD_GROUPS * SSD_STATE].T.astype(bt_ref.dtype)

        cpr = half // S5_CHUNK
        for k in range(n_slab):
            us_ref[r, k] = u[:, k * LANES:(k + 1) * LANES]
        for s in range(S5_CHUNK):
            for k in range(n_slab):
                u_ref[r * cpr:(r + 1) * cpr, s * S5_DIM + k * LANES:s * S5_DIM + (k + 1) * LANES] = \
                    us_ref[r, k, pl.ds(s, cpr, stride=S5_CHUNK), :]


def _inproj(x, g, b, w, conv_w, conv_b, dt_bias, t, apply_ln):
    n = x.shape[0]
    tm, hl = IN_ROW_TILE, PROJ_HALO
    per = tm // hl
    row = lambda wd: pl.BlockSpec((tm, wd), lambda i: (i, 0))
    bn = SSD_GROUPS * SSD_STATE
    out_shape = [jax.ShapeDtypeStruct((n, 3 * NA_DIM), BF16), jax.ShapeDtypeStruct((n, SSD_DIM), F32),
                 jax.ShapeDtypeStruct((n, SSD_CONV_DIM), BF16), jax.ShapeDtypeStruct((bn, n), BF16),
                 jax.ShapeDtypeStruct((n // S5_CHUNK, S5_CHUNK * S5_DIM), F32),
                 jax.ShapeDtypeStruct((2 * SSD_HEADS, n), F32)]
    out_specs = [row(3 * NA_DIM), row(SSD_DIM), row(SSD_CONV_DIM), pl.BlockSpec((bn, tm), lambda i: (0, i)),
                 pl.BlockSpec((tm // S5_CHUNK, S5_CHUNK * S5_DIM), lambda i: (i, 0)),
                 pl.BlockSpec((2 * SSD_HEADS, tm), lambda i: (0, i))]
    if apply_ln:
        out_shape.append(jax.ShapeDtypeStruct((n, D_MODEL), F32))
        out_specs.append(row(D_MODEL))
    return pl.pallas_call(
        functools.partial(_inproj_kernel, apply_ln=apply_ln, tiles_per_seq=t // tm),
        grid=(n // tm,),
        in_specs=[row(D_MODEL),
                  pl.BlockSpec((hl, D_MODEL), lambda i: (jnp.maximum(i * per - 1, 0), 0)),
                  pl.BlockSpec((hl, D_MODEL), lambda i: (jnp.minimum((i + 1) * per, n // hl - 1), 0)),
                  _const_spec((1, D_MODEL)), _const_spec((1, D_MODEL)), _const_spec(w.shape),
                  _const_spec((SSD_CONV, SSD_CONV_DIM)), _const_spec((1, SSD_CONV_DIM)), _const_spec((1, DT_PAD))],
        out_specs=out_specs, out_shape=out_shape,
        scratch_shapes=[pltpu.VMEM((tm + 2 * hl, D_MODEL), BF16),
                        pltpu.VMEM((2, tm // 2 + 2 * hl, SSD_CONV_DIM), F32),
                        pltpu.VMEM((2, S5_DIM // LANES, tm // 2, LANES), F32)],
        compiler_params=_params("parallel"), name="in_proj",
    )(x, x, x, g, b, w, conv_w, conv_b, dt_bias)


def _na_bias_tables(rpb, rows):
    n_ri, n_ci = 2 * NA_WIN_ROWS - 1, 2 * NA_WIN_COLS - 1
    qr = np.arange(NA_Q_ROWS)
    kr = np.arange(NA_K_ROWS)
    c = np.arange(GRID_W)
    c0 = np.clip(c - NA_WIN_COLS // 2, 0, GRID_W - NA_WIN_COLS)
    col_ok = (c[None, :] >= c0[:, None]) & (c[None, :] < c0[:, None] + NA_WIN_COLS)
    ci = np.clip(c[None, :] - c[:, None], -(NA_WIN_COLS - 1), NA_WIN_COLS - 1) + (NA_WIN_COLS - 1)
    onehot_ci = (ci[None] == np.arange(n_ci)[:, None, None]).astype(np.float32)
    onehot_ri, ok = [], []
    for blk_row in (0, NA_Q_ROWS, rows - NA_Q_ROWS):
        start = min(max(blk_row - NA_WIN_ROWS // 2, 0), rows - NA_K_ROWS)
        r = blk_row + qr
        r0 = np.clip(r - NA_WIN_ROWS // 2, 0, rows - NA_WIN_ROWS)
        key_row = start + kr
        row_ok = (key_row[None, :] >= r0[:, None]) & (key_row[None, :] < r0[:, None] + NA_WIN_ROWS)
        ri = key_row[None, :] - r[:, None] + (NA_WIN_ROWS - 1)
        onehot_ri.append(((ri[..., None] == np.arange(n_ri)) & row_ok[..., None]).astype(np.float32))
        ok.append(row_ok[:, None, :, None] & col_ok[None, :, None, :])
    col_tab = jnp.einsum('hrc,cqk->hrqk', rpb, jnp.asarray(onehot_ci), precision=HIGHEST)
    bias = jnp.einsum('aqjr,hrwk->ahqwjk', jnp.asarray(np.stack(onehot_ri)), col_tab, precision=HIGHEST)
    bias = jnp.where(jnp.asarray(np.stack(ok))[:, None], bias, NEG_BIG)
    return bias.reshape(3, NA_HEADS, NA_Q_ROWS * GRID_W, NA_K_ROWS * GRID_W).astype(F32)


def _na_kernel(q_ref, k_ref, v_ref, bias_ref, o_ref, *, rows):
    nblk = rows // NA_Q_ROWS
    nk = NA_K_ROWS * GRID_W
    tq = NA_Q_ROWS * GRID_W
    for r in range(NA_BLOCKS_PER_STEP):
        i = pl.program_id(1) * NA_BLOCKS_PER_STEP + r
        case = jnp.where(i == 0, 0, jnp.where(i == nblk - 1, 2, 1))
        start_row = jnp.clip(i * NA_Q_ROWS - NA_WIN_ROWS // 2, 0, rows - NA_K_ROWS)
        start = pl.multiple_of(start_row * GRID_W, GRID_W)
        rs = slice(r * tq, (r + 1) * tq)
        q = q_ref[rs, :] * (HEAD_DIM ** -0.5)
        kw = k_ref[pl.ds(start, nk), :]
        vw = v_ref[pl.ds(start, nk), :]
        for h in range(NA_HEADS):
            sl = slice(h * HEAD_DIM, (h + 1) * HEAD_DIM)
            s = lax.dot_general(q[:, sl], kw[:, sl], (((1,), (1,)), ((), ())), preferred_element_type=F32)
            s = s + bias_ref[case, h]
            m = jnp.max(s, axis=-1, keepdims=True)
            p = jnp.exp(s - m)
            l = jnp.sum(p, axis=-1, keepdims=True)
            o = jnp.dot(p.astype(BF16), vw[:, sl], preferred_element_type=F32)
            o_ref[rs, sl] = (o / l).astype(o_ref.dtype)


def _na(qkv, bias, bsz, t):
    rows = t // GRID_W
    tq = NA_BLOCKS_PER_STEP * NA_Q_ROWS * GRID_W
    qkv3 = qkv.reshape(bsz, t, 3 * NA_DIM)
    out = pl.pallas_call(
        functools.partial(_na_kernel, rows=rows),
        grid=(bsz, t // tq),
        in_specs=[pl.BlockSpec((None, tq, NA_DIM), lambda b, i: (b, i, 0)),
                  pl.BlockSpec((None, t, NA_DIM), lambda b, i: (b, 0, 1)),
                  pl.BlockSpec((None, t, NA_DIM), lambda b, i: (b, 0, 2)),
                  pl.BlockSpec(bias.shape, lambda b, i: (0, 0, 0, 0), pipeline_mode=pl.Buffered(1))],
        out_specs=pl.BlockSpec((None, tq, NA_DIM), lambda b, i: (b, i, 0)),
        out_shape=jax.ShapeDtypeStruct((bsz, t, NA_DIM), BF16),
        compiler_params=_params("parallel", "arbitrary"), name="na_attn",
    )(qkv3, qkv3, qkv3, bias)
    return out.reshape(bsz * t, NA_DIM)


def _tri(n, lower):
    r = lax.broadcasted_iota(jnp.int32, (n, n), 0)
    c = lax.broadcasted_iota(jnp.int32, (n, n), 1)
    return (r >= c) if lower else (r <= c)


def _split3(x):
    hi = x.astype(BF16).astype(F32)
    r = x - hi
    mid = r.astype(BF16).astype(F32)
    lo = (r - mid).astype(BF16).astype(F32)
    return hi, mid, lo


def _dot01_left(m01, x):
    return sum(jnp.dot(m01, p.astype(BF16), preferred_element_type=F32) for p in _split3(x))


def _dot01_right(x, m01):
    return sum(jnp.dot(p.astype(BF16), m01, preferred_element_type=F32) for p in _split3(x))


def _head_expand():
    r = lax.broadcasted_iota(jnp.int32, (SSD_HEADS, SSD_DIM), 0)
    c = lax.broadcasted_iota(jnp.int32, (SSD_HEADS, SSD_DIM), 1)
    return (c // HEAD_DIM == r).astype(BF16)


def _ssd_state_kernel(xf_ref, xb_ref, btf_ref, btb_ref, dttf_ref, dttb_ref, alog_ref,
                      pf_ref, pb_ref, sf_ref, sb_ref):
    q = SSD_CHUNK
    nh = SSD_HEADS
    pair_w = 2 * HEAD_DIM

    @pl.when(pl.program_id(1) == 0)
    def _():
        sf_ref[...] = jnp.zeros_like(sf_ref)
        sb_ref[...] = jnp.zeros_like(sb_ref)

    row_id = lax.broadcasted_iota(jnp.int32, (q, q), 0)
    col_id = lax.broadcasted_iota(jnp.int32, (q, q), 1)
    first_half = lax.broadcasted_iota(jnp.int32, (SSD_STATE, pair_w), 1) < HEAD_DIM
    a_col = -jnp.exp(alog_ref[...])
    for d, (x_ref, bt_ref, dtt_ref, prev_ref, state_ref) in enumerate(
            ((xf_ref, btf_ref, dttf_ref, pf_ref, sf_ref), (xb_ref, btb_ref, dttb_ref, pb_ref, sb_ref))):
        backward = d == 1
        hs = slice(d * nh, (d + 1) * nh)
        tri = ((row_id >= col_id) if backward else (row_id <= col_id)).astype(BF16)
        dtr = dtt_ref[hs, :]
        da = dtr * a_col[hs]
        local, decay = [], []
        da_rows = jnp.concatenate([da[:, cc * q:(cc + 1) * q] for cc in range(SSD_BLOCK_CHUNKS)], axis=0)
        cs_rows = _dot01_right(da_rows, tri)
        for cc in range(SSD_BLOCK_CHUNKS):
            rs = slice(cc * q, (cc + 1) * q)
            cs = cs_rows[cc * nh:(cc + 1) * nh]
            tot = cs[:, 0:1] if backward else cs[:, q - 1:q]
            w = jnp.exp(tot - cs) * dtr[:, rs]
            chunk_decay = jnp.exp(tot)
            local.append([])
            decay.append([])
            for pp in range(nh // 2):
                g = (2 * pp) // (nh // SSD_GROUPS)
                bt = bt_ref[g * SSD_STATE:(g + 1) * SSD_STATE, rs].astype(F32)
                xs = x_ref[rs, pp * pair_w:(pp + 1) * pair_w]
                lhs = jnp.concatenate([(bt * w[h:h + 1, :]).astype(BF16) for h in (2 * pp, 2 * pp + 1)], axis=0)
                both = jnp.dot(lhs, xs, preferred_element_type=F32)
                local[cc].append(jnp.where(first_half, both[:SSD_STATE], both[SSD_STATE:]))
                decay[cc].append(jnp.where(first_half, chunk_decay[2 * pp:2 * pp + 1, :],
                                           chunk_decay[2 * pp + 1:2 * pp + 2, :]))
        order = range(SSD_BLOCK_CHUNKS - 1, -1, -1) if backward else range(SSD_BLOCK_CHUNKS)
        for pp in range(nh // 2):
            lanes = slice(pp * pair_w, (pp + 1) * pair_w)
            state = state_ref[:, lanes]
            for cc in order:
                prev_ref[cc, :, lanes] = state.astype(prev_ref.dtype)
                state = state * decay[cc][pp] + local[cc][pp]
            state_ref[:, lanes] = state


def _ssd_states(xc3, bt, dtt, a_log16):
    bsz, t, _ = xc3.shape
    tb = SSD_BLOCK_CHUNKS * SSD_CHUNK
    nblk = t // tb
    bn = SSD_GROUPS * SSD_STATE
    fwd_rows = pl.BlockSpec((None, tb, SSD_CONV_DIM), lambda b, j: (b, j, 0))
    bwd_rows = pl.BlockSpec((None, tb, SSD_CONV_DIM), lambda b, j: (b, nblk - 1 - j, 0))
    fwd_cols = lambda rows: pl.BlockSpec((rows, tb), lambda b, j: (0, b * nblk + j))
    bwd_cols = lambda rows: pl.BlockSpec((rows, tb), lambda b, j: (0, b * nblk + nblk - 1 - j))
    state = jax.ShapeDtypeStruct((bsz, t // SSD_CHUNK, SSD_STATE, SSD_DIM), BF16)
    return pl.pallas_call(
        _ssd_state_kernel, grid=(bsz, nblk),
        in_specs=[fwd_rows, bwd_rows, fwd_cols(bn), bwd_cols(bn), fwd_cols(2 * SSD_HEADS), bwd_cols(2 * SSD_HEADS),
                  _const_spec((2 * SSD_HEADS, 1))],
        out_specs=[pl.BlockSpec((None, SSD_BLOCK_CHUNKS, SSD_STATE, SSD_DIM), lambda b, j: (b, j, 0, 0)),
                   pl.BlockSpec((None, SSD_BLOCK_CHUNKS, SSD_STATE, SSD_DIM),
                                lambda b, j: (b, nblk - 1 - j, 0, 0))],
        out_shape=[state, state],
        scratch_shapes=[pltpu.VMEM((SSD_STATE, SSD_DIM), F32), pltpu.VMEM((SSD_STATE, SSD_DIM), F32)],
        compiler_params=_params("parallel", "arbitrary"), name="ssd_state",
    )(xc3, xc3, bt, bt, dtt, dtt, a_log16.reshape(2 * SSD_HEADS, 1))


def _ssd_out_kernel(x_ref, z_ref, dtt_ref, pf_ref, pb_ref, alog_col_ref, dskip_ref, nw_ref, o_ref):
    q = SSD_CHUNK
    nh = SSD_HEADS
    row_id = lax.broadcasted_iota(jnp.int32, (q, q), 0)
    col_id = lax.broadcasted_iota(jnp.int32, (q, q), 1)
    lower = row_id >= col_id
    eye = row_id == col_id
    first_half = lax.broadcasted_iota(jnp.int32, (q, 2 * HEAD_DIM), 1) < HEAD_DIM
    tri_l = lower.astype(BF16)
    tri_u = (row_id <= col_id).astype(BF16)
    a_col = -jnp.exp(alog_col_ref[...])
    dtt_all = dtt_ref[...]
    da = dtt_all * a_col
    da_rows = jnp.concatenate([da[:, cc * q:(cc + 1) * q] for cc in range(SSD_BLOCK_CHUNKS)], axis=0)
    cs_rows_f, cs_rows_b = _dot01_right(da_rows, tri_u), _dot01_right(da_rows, tri_l)
    cs_cols_f, cs_cols_b = cs_rows_f.T, cs_rows_b.T
    e_cols_f, e_cols_b = jnp.exp(cs_cols_f), jnp.exp(cs_cols_b)
    log_dt = jnp.log(dtt_all)
    ldt_rows = jnp.concatenate([log_dt[:, cc * q:(cc + 1) * q] for cc in range(SSD_BLOCK_CHUNKS)], axis=0)
    rp_rows_f, rp_rows_b = cs_rows_f - ldt_rows, cs_rows_b - ldt_rows
    for cc in range(SSD_BLOCK_CHUNKS):
        rs = slice(cc * q, (cc + 1) * q)
        dtr = dtt_all[:, rs]
        hrow = slice(cc * 2 * nh, (cc + 1) * 2 * nh)
        cs_row_f, cs_row_b = rp_rows_f[hrow], rp_rows_b[hrow]
        cs_col_f, cs_col_b = cs_cols_f[:, hrow], cs_cols_b[:, hrow]
        e_col_f, e_col_b = e_cols_f[:, hrow], e_cols_b[:, hrow]
        bc = x_ref[rs, SSD_DIM:SSD_CONV_DIM]
        groups = []
        for g in range(SSD_GROUPS):
            bg = bc[:, g * SSD_STATE:(g + 1) * SSD_STATE]
            cg = bc[:, (SSD_GROUPS + g) * SSD_STATE:(SSD_GROUPS + g + 1) * SSD_STATE]
            cb = lax.dot_general(cg, bg, (((1,), (1,)), ((), ())), preferred_element_type=F32)
            c2 = jnp.concatenate([cg, cg], axis=1).astype(F32)
            pairs = []
            for pp in range(SSD_HEADS // SSD_GROUPS // 2):
                h0 = g * (SSD_HEADS // SSD_GROUPS) + 2 * pp
                lanes = slice(h0 * HEAD_DIM, (h0 + 2) * HEAD_DIM)
                rhs = jnp.concatenate([x_ref[rs, lanes], pf_ref[cc, :, lanes], pb_ref[cc, :, lanes]], axis=0)
                lhs = []
                for h in (h0, h0 + 1):
                    seg_f = cs_col_f[:, h:h + 1] - cs_row_f[h:h + 1, :]
                    seg_b = cs_col_b[:, nh + h:nh + h + 1] - cs_row_b[nh + h:nh + h + 1, :]
                    lmat = (jnp.exp(jnp.where(lower, seg_f, seg_b))
                            + jnp.where(eye, dtr[nh + h:nh + h + 1, :], 0.0))
                    e2 = jnp.where(first_half, e_col_f[:, h:h + 1], e_col_b[:, nh + h:nh + h + 1])
                    lhs.append(jnp.concatenate([(cb * lmat).astype(BF16), (c2 * e2).astype(BF16)], axis=1))
                both = jnp.dot(jnp.concatenate(lhs, axis=0), rhs, preferred_element_type=F32)
                pairs.append(jnp.where(first_half, both[:q], both[q:]))
            groups.append(jnp.concatenate(pairs, axis=1))
        y = jnp.concatenate(groups, axis=1) + dskip_ref[...] * x_ref[rs, :SSD_DIM].astype(F32)
        yg = y * _silu(z_ref[rs, :])
        ms = jnp.mean(yg * yg, axis=-1, keepdims=True)
        o_ref[rs, :] = (yg * lax.rsqrt(ms + LN_EPS) * nw_ref[...]).astype(o_ref.dtype)


def _ssd_out(xc3, z3, dtt, prev_f, prev_b, a_log16, d_skip, norm_w):
    bsz, t, _ = xc3.shape
    tb = SSD_BLOCK_CHUNKS * SSD_CHUNK
    nblk = t // tb
    blk = lambda wd: pl.BlockSpec((None, tb, wd), lambda b, j: (b, j, 0))
    prev = pl.BlockSpec((None, SSD_BLOCK_CHUNKS, SSD_STATE, SSD_DIM), lambda b, j: (b, j, 0, 0))
    out = pl.pallas_call(
        _ssd_out_kernel, grid=(bsz, nblk),
        in_specs=[blk(SSD_CONV_DIM), blk(SSD_DIM),
                  pl.BlockSpec((2 * SSD_HEADS, tb), lambda b, j: (0, b * nblk + j)),
                  prev, prev, _const_spec((2 * SSD_HEADS, 1)),
                  _const_spec((1, SSD_DIM)), _const_spec((1, SSD_DIM))],
        out_specs=blk(SSD_DIM),
        out_shape=jax.ShapeDtypeStruct((bsz, t, SSD_DIM), BF16),
        compiler_params=_params("parallel", "parallel"), name="ssd_out",
    )(xc3, z3, dtt, prev_f, prev_b, a_log16.reshape(2 * SSD_HEADS, 1), d_skip, norm_w)
    return out.reshape(bsz * t, SSD_DIM)


def _s5_tables(lam_re, lam_im, log_dt, b_re, b_im, c_re, c_im):
    lc = S5_CHUNK
    hp = HIGHEST
    pw_re, pw_im, bb_re, bb_im = [], [], [], []
    for d in range(2):
        dt = jnp.exp(log_dt[d])[:, None]
        lr, li = lam_re[d], lam_im[d]
        mag = jnp.exp(lr * dt)
        ar, ai = mag * jnp.cos(li * dt), mag * jnp.sin(li * dt)
        den = lr * lr + li * li
        fr = ((ar - 1.0) * lr + ai * li) / den
        fi = (ai * lr - (ar - 1.0) * li) / den
        bb_re.append(fr[..., None] * b_re[d] - fi[..., None] * b_im[d])
        bb_im.append(fr[..., None] * b_im[d] + fi[..., None] * b_re[d])
        k = jnp.arange(lc + 1, dtype=F32)[:, None, None]
        pw_re.append(jnp.exp(k * (lr * dt)) * jnp.cos(k * (li * dt)))
        pw_im.append(jnp.exp(k * (lr * dt)) * jnp.sin(k * (li * dt)))

    def kern(d):
        wr = pw_re[d][:lc, :, :, None] * bb_re[d][None] - pw_im[d][:lc, :, :, None] * bb_im[d][None]
        wi = pw_re[d][:lc, :, :, None] * bb_im[d][None] + pw_im[d][:lc, :, :, None] * bb_re[d][None]
        return (jnp.einsum('ghp,kgpj->kghj', c_re[d], wr, precision=hp)
                - jnp.einsum('ghp,kgpj->kghj', c_im[d], wi, precision=hp))

    kf, kb = kern(0), kern(1)
    l = jnp.arange(lc)
    lag = l[:, None] - l[None, :]
    tf = jnp.where((lag >= 0)[:, :, None, None, None], kf[jnp.clip(lag, 0, lc - 1)], 0.0)
    tb = jnp.where((lag <= 0)[:, :, None, None, None], kb[jnp.clip(-lag, 0, lc - 1)], 0.0)
    toep = (tf + tb).transpose(2, 1, 4, 0, 3).reshape(S5_GROUPS, lc * S5_GROUP_CH, lc * S5_GROUP_CH)

    def state_in(d, powers):
        wr = pw_re[d][powers][:, :, :, None] * bb_re[d][None] - pw_im[d][powers][:, :, :, None] * bb_im[d][None]
        wi = pw_re[d][powers][:, :, :, None] * bb_im[d][None] + pw_im[d][powers][:, :, :, None] * bb_re[d][None]
        to_rows = lambda w: w.transpose(1, 0, 3, 2).reshape(S5_GROUPS, lc * S5_GROUP_CH, S5_STATE)
        return to_rows(wr), to_rows(wi)

    f_re, f_im = state_in(0, lc - 1 - l)
    b_re_, b_im_ = state_in(1, l)
    m_state = jnp.concatenate([f_re, b_re_, f_im, b_im_], axis=-1)

    def state_out(d, powers):
        cpr = c_re[d][None] * pw_re[d][powers][:, :, None, :] - c_im[d][None] * pw_im[d][powers][:, :, None, :]
        cpi = c_re[d][None] * pw_im[d][powers][:, :, None, :] + c_im[d][None] * pw_re[d][powers][:, :, None, :]
        to_cols = lambda w: w.transpose(1, 3, 0, 2).reshape(S5_GROUPS, S5_STATE, lc * S5_GROUP_CH)
        return to_cols(cpr), -to_cols(cpi)

    of_re, of_im = state_out(0, l + 1)
    ob_re, ob_im = state_out(1, lc - l)
    m_off = jnp.concatenate([of_re, ob_re, of_im, ob_im], axis=1)
    dec_re = jnp.concatenate([pw_re[0][lc], pw_re[1][lc]], axis=-1)[:, None, :]
    dec_im = jnp.concatenate([pw_im[0][lc], pw_im[1][lc]], axis=-1)[:, None, :]
    return toep.astype(BF16), m_state.astype(BF16), m_off.astype(BF16), dec_re, dec_im


S5_W = S5_CHUNK * S5_GROUP_CH
S5_RELAYOUT_CHUNKS = 64
S5_GROUPS_PER_STEP = 2


def _s5_group_kernel(u_ref, o_ref):
    x = u_ref[...]
    for g in range(S5_GROUPS):
        lo = g * S5_GROUP_CH
        pieces = [x[:, s * S5_DIM + lo:s * S5_DIM + lo + S5_GROUP_CH] for s in range(S5_CHUNK)]
        o_ref[g] = jnp.concatenate(pieces, axis=-1).astype(o_ref.dtype)


def _s5_ungroup_kernel(y_ref, u_ref, d_ref, o_ref):
    ys = [y_ref[g] for g in range(S5_GROUPS)]
    for l in range(S5_CHUNK):
        cols = slice(l * S5_DIM, (l + 1) * S5_DIM)
        pieces = [y[:, l * S5_GROUP_CH:(l + 1) * S5_GROUP_CH] for y in ys]
        o_ref[:, cols] = jnp.concatenate(pieces, axis=-1) + d_ref[...] * u_ref[:, cols]


def _s5_kernel(u_ref, toep_ref, mst_ref, moff_ref, are_ref, aim_ref, y_ref, s_ref, e_ref, *, nc, bsz):
    ng = S5_GROUPS_PER_STEP
    half = S5_STATE
    w = S5_W
    for gi in range(ng):
        for b in range(bsz):
            ub = u_ref[gi, :, b * w:(b + 1) * w]
            y_ref[gi, :, b * w:(b + 1) * w] = jnp.dot(ub, toep_ref[gi], preferred_element_type=F32)
            s = jnp.dot(ub, mst_ref[gi], preferred_element_type=F32)
            for k in range(2):
                s_ref[gi, k, pl.ds(b, nc, stride=bsz), :] = s[:, k * LANES:(k + 1) * LANES]
    ar = [jnp.broadcast_to(are_ref[gi], (bsz, 2 * half)) for gi in range(ng)]
    ai = [jnp.broadcast_to(aim_ref[gi], (bsz, 2 * half)) for gi in range(ng)]
    is_fwd = lax.broadcasted_iota(jnp.int32, (bsz, 2 * half), 1) < half

    def body(i, carry):
        rf = pl.multiple_of(i * bsz, bsz)
        rb = pl.multiple_of((nc - 1 - i) * bsz, bsz)
        out = []
        for gi in range(ng):
            er, ei = carry[gi]
            e_ref[gi, 0, pl.ds(rf, bsz), 0:half] = er[:, :half]
            e_ref[gi, 0, pl.ds(rb, bsz), half:] = er[:, half:]
            e_ref[gi, 1, pl.ds(rf, bsz), 0:half] = ei[:, :half]
            e_ref[gi, 1, pl.ds(rb, bsz), half:] = ei[:, half:]
            sr = jnp.where(is_fwd, s_ref[gi, 0, pl.ds(rf, bsz), :], s_ref[gi, 0, pl.ds(rb, bsz), :])
            si = jnp.where(is_fwd, s_ref[gi, 1, pl.ds(rf, bsz), :], s_ref[gi, 1, pl.ds(rb, bsz), :])
            out.append((ar[gi] * er - ai[gi] * ei + sr, ar[gi] * ei + ai[gi] * er + si))
        return tuple(out)

    zero = jnp.zeros((bsz, 2 * half), F32)
    lax.fori_loop(0, nc, body, tuple((zero, zero) for _ in range(ng)), unroll=2)
    for gi in range(ng):
        for b in range(bsz):
            e = jnp.concatenate([e_ref[gi, k, pl.ds(b, nc, stride=bsz), :] for k in range(2)], axis=-1)
            y_ref[gi, :, b * w:(b + 1) * w] += jnp.dot(e.astype(BF16), moff_ref[gi], preferred_element_type=F32)


def _s5(u_cm, tables, d_skip, bsz, t):
    toep, m_state, m_off, dec_re, dec_im = tables
    nc = t // S5_CHUNK
    w = S5_W
    rc = min(S5_RELAYOUT_CHUNKS, nc)
    u3 = u_cm.reshape(bsz, nc, S5_CHUNK * S5_DIM)
    tok = pl.BlockSpec((None, rc, S5_CHUNK * S5_DIM), lambda b, j: (b, j, 0))
    grouped = pl.BlockSpec((S5_GROUPS, rc, w), lambda b, j: (0, j, b))
    ug = pl.pallas_call(
        _s5_group_kernel, grid=(bsz, nc // rc), in_specs=[tok], out_specs=grouped,
        out_shape=jax.ShapeDtypeStruct((S5_GROUPS, nc, bsz * w), BF16),
        compiler_params=_params("parallel", "parallel"), name="s5_group",
    )(u3)
    ng = S5_GROUPS_PER_STEP
    grp = lambda shape: pl.BlockSpec((ng,) + shape, lambda g: (g, 0, 0))
    yg = pl.pallas_call(
        functools.partial(_s5_kernel, nc=nc, bsz=bsz),
        grid=(S5_GROUPS // ng,),
        in_specs=[grp((nc, bsz * w)), grp((w, w)), grp((w, w)), grp((w, w)),
                  grp((1, 2 * S5_STATE)), grp((1, 2 * S5_STATE))],
        out_specs=grp((nc, bsz * w)),
        out_shape=jax.ShapeDtypeStruct((S5_GROUPS, nc, bsz * w), F32),
        scratch_shapes=[pltpu.VMEM((ng, 2, nc * bsz, LANES), F32), pltpu.VMEM((ng, 2, nc * bsz, LANES), F32)],
        compiler_params=_params("parallel"), name="s5_scan",
    )(ug, toep, m_state, m_off, dec_re, dec_im)
    y = pl.pallas_call(
        _s5_ungroup_kernel, grid=(bsz, nc // rc),
        in_specs=[grouped, tok, pl.BlockSpec((1, S5_DIM), lambda b, j: (0, 0))], out_specs=tok,
        out_shape=jax.ShapeDtypeStruct((bsz, nc, S5_CHUNK * S5_DIM), F32),
        compiler_params=_params("parallel", "parallel"), name="s5_ungroup",
    )(yg, u3, d_skip)
    return y.reshape(bsz * nc, S5_CHUNK * S5_DIM)


def _mixout_kernel(h_ref, na_ref, ssd_ref, y5_ref, gw_ref, gb_ref, wo_ref, g_ref, b_ref, o_ref, ys_ref):
    tm = h_ref.shape[0]
    n_slab = S5_DIM // LANES
    for l in range(S5_CHUNK):
        for k in range(n_slab):
            ys_ref[k, pl.ds(l, tm // S5_CHUNK, stride=S5_CHUNK), :] = \
                y5_ref[:, l * S5_DIM + k * LANES:l * S5_DIM + (k + 1) * LANES]
    half = tm // 2
    for r in range(2):
        rs = slice(r * half, (r + 1) * half)
        y5 = jnp.concatenate([ys_ref[k, rs, :] for k in range(n_slab)], axis=-1)
        gl = 0.5 * y5 * (1.0 + jnp.tanh(math.sqrt(2.0 / math.pi) * (y5 + 0.044715 * (y5 * y5 * y5))))
        gate = jnp.dot(gl.astype(BF16), gw_ref[...], preferred_element_type=F32) + gb_ref[...]
        o5 = gl * _sigmoid(gate)
        mixed = jnp.concatenate([na_ref[rs, :], ssd_ref[rs, :], o5.astype(BF16)], axis=-1)
        mix = jnp.dot(mixed, wo_ref[...], preferred_element_type=F32)
        o_ref[rs, :] = _layernorm(DEEPNORM_ALPHA * h_ref[rs, :] + mix, g_ref[...], b_ref[...])


def _mixout(h, o_na, o_ssd, y5_cm, glu_w, glu_b, w_out, g, b):
    n = h.shape[0]
    tm = MIX_ROW_TILE
    row = lambda wd: pl.BlockSpec((tm, wd), lambda i: (i, 0))
    return pl.pallas_call(
        _mixout_kernel, grid=(n // tm,),
        in_specs=[row(D_MODEL), row(NA_DIM), row(SSD_DIM),
                  pl.BlockSpec((tm // S5_CHUNK, S5_CHUNK * S5_DIM), lambda i: (i, 0)),
                  _const_spec((S5_DIM, S5_DIM)), _const_spec((1, S5_DIM)),
                  _const_spec((D_MODEL, D_MODEL)), _const_spec((1, D_MODEL)), _const_spec((1, D_MODEL))],
        out_specs=row(D_MODEL), out_shape=jax.ShapeDtypeStruct((n, D_MODEL), F32),
        scratch_shapes=[pltpu.VMEM((S5_DIM // LANES, tm, LANES), F32)],
        compiler_params=_params("parallel"), name="mix_out",
    )(h, o_na, o_ssd, y5_cm, glu_w, glu_b, w_out, g, b)


def _kvproj_kernel(m_ref, wk_ref, wv_ref, k_ref, v_ref):
    mb = m_ref[...].astype(BF16)
    k_ref[...] = jnp.dot(mb, wk_ref[...], preferred_element_type=F32).astype(k_ref.dtype)
    v_ref[...] = jnp.dot(mb, wv_ref[...], preferred_element_type=F32).astype(v_ref.dtype)


def _kvproj(mem2, wk, wv):
    n = mem2.shape[0]
    tm = min(ROW_TILE, n)
    row = pl.BlockSpec((tm, D_MODEL), lambda i: (i, 0))
    return pl.pallas_call(
        _kvproj_kernel, grid=(n // tm,),
        in_specs=[row, _const_spec((D_MODEL, D_MODEL)), _const_spec((D_MODEL, D_MODEL))],
        out_specs=[row, row], out_shape=[jax.ShapeDtypeStruct((n, D_MODEL), BF16)] * 2,
        compiler_params=_params("parallel"), name="xa_kv_proj",
    )(mem2, wk, wv)


def _xattn_kernel(h_ref, k_ref, v_ref, wq_ref, wo_ref, g_ref, b_ref, o_ref, oh_ref):
    half = h_ref.shape[0] // 2
    for r in range(2):
        rs = slice(r * half, (r + 1) * half)
        h = h_ref[rs, :]
        q = jnp.dot(h.astype(BF16), wq_ref[...], preferred_element_type=F32) * (XA_HEAD_DIM ** -0.5)
        qb = q.astype(BF16)
        for hd in range(XA_HEADS):
            sl = slice(hd * XA_HEAD_DIM, (hd + 1) * XA_HEAD_DIM)
            s = lax.dot_general(qb[:, sl], k_ref[:, sl], (((1,), (1,)), ((), ())), preferred_element_type=F32)
            m = jnp.max(s, axis=-1, keepdims=True)
            p = jnp.exp(s - m)
            l = jnp.sum(p, axis=-1, keepdims=True)
            o = jnp.dot(p.astype(BF16), v_ref[:, sl], preferred_element_type=F32) / l
            oh_ref[rs, sl] = o.astype(oh_ref.dtype)
        xa = jnp.dot(oh_ref[rs, :], wo_ref[...], preferred_element_type=F32)
        o_ref[rs, :] = _layernorm(DEEPNORM_ALPHA * h + xa, g_ref[...], b_ref[...])


def _xattn(h, k3, v3, wq, wo, g, b, bsz, t):
    m = k3.shape[1]
    h3 = h.reshape(bsz, t, D_MODEL)
    tm = XA_ROW_TILE
    row = pl.BlockSpec((None, tm, D_MODEL), lambda bb, i: (bb, i, 0))
    kv = pl.BlockSpec((None, m, D_MODEL), lambda bb, i: (bb, 0, 0))
    const = lambda shape: pl.BlockSpec(shape, lambda bb, i: (0, 0))
    out = pl.pallas_call(
        _xattn_kernel, grid=(bsz, t // tm),
        in_specs=[row, kv, kv, const((D_MODEL, D_MODEL)), const((D_MODEL, D_MODEL)),
                  const((1, D_MODEL)), const((1, D_MODEL))],
        out_specs=row, out_shape=jax.ShapeDtypeStruct((bsz, t, D_MODEL), F32),
        scratch_shapes=[pltpu.VMEM((tm, D_MODEL), BF16)],
        compiler_params=_params("parallel", "parallel"), name="cross_attn",
    )(h3, k3, v3, wq, wo, g, b)
    return out.reshape(bsz * t, D_MODEL)


FF_TILE = 1024


def _mlp_kernel(h_ref, w1_ref, w2_ref, g_ref, b_ref, o_ref, a_ref):
    half = h_ref.shape[0] // 2
    for r in range(2):
        rs = slice(r * half, (r + 1) * half)
        h = h_ref[rs, :]
        hb = h.astype(BF16)
        for c in range(D_FF // FF_TILE):
            sl = slice(c * FF_TILE, (c + 1) * FF_TILE)
            a = jnp.maximum(jnp.dot(hb, w1_ref[:, sl], preferred_element_type=F32), 0.0)
            a_ref[rs, sl] = (a * a).astype(a_ref.dtype)
        ff = jnp.dot(a_ref[rs, :], w2_ref[...], preferred_element_type=F32)
        o_ref[rs, :] = _layernorm(DEEPNORM_ALPHA * h + ff, g_ref[...], b_ref[...])


def _mlp(h, w1, w2, g, b):
    n = h.shape[0]
    row = pl.BlockSpec((MLP_ROW_TILE, D_MODEL), lambda i: (i, 0))
    return pl.pallas_call(
        _mlp_kernel, grid=(n // MLP_ROW_TILE,),
        in_specs=[row,
                  pl.BlockSpec((D_MODEL, D_FF), lambda i: (0, 0), pipeline_mode=pl.Buffered(1)),
                  pl.BlockSpec((D_FF, D_MODEL), lambda i: (0, 0), pipeline_mode=pl.Buffered(1)),
                  _const_spec((1, D_MODEL)), _const_spec((1, D_MODEL))],
        out_specs=row, out_shape=jax.ShapeDtypeStruct((n, D_MODEL), F32),
        scratch_shapes=[pltpu.VMEM((MLP_ROW_TILE, D_FF), BF16)],
        compiler_params=_params("parallel"), name="mlp",
    )(h, w1, w2, g, b)


def _permute_w_in(w):
    dt_lo = 3 * NA_DIM + SSD_DIM + SSD_CONV_DIM
    dt_hi = dt_lo + 2 * SSD_HEADS
    pad = jnp.zeros((w.shape[0], DT_PAD - 2 * SSD_HEADS), w.dtype)
    return jnp.concatenate([w[:, :dt_lo], w[:, dt_hi:], w[:, dt_lo:dt_hi], pad], axis=1).astype(BF16)


def kernel(x, mem, ln_in_g, ln_in_b, w_in, na_rpb, ssd_conv_w, ssd_conv_b, ssd_dt_bias, ssd_a_log, ssd_d,
           ssd_norm_w, s5_lam_re, s5_lam_im, s5_log_dt, s5_b_re, s5_b_im, s5_c_re, s5_c_im, s5_d, s5_glu_w,
           s5_glu_b, w_mix_out, ln_mix_g, ln_mix_b, xa_wq, xa_wk, xa_wv, xa_wo, ln_xa_g, ln_xa_b, mlp_w1,
           mlp_w2, ln_mlp_g, ln_mlp_b):
    bsz, t, _ = x.shape
    n = bsz * t
    rows = t // GRID_W
    assert all(t % tile == 0 for tile in (IN_ROW_TILE, MIX_ROW_TILE, XA_ROW_TILE, MLP_ROW_TILE))
    assert t % (SSD_BLOCK_CHUNKS * SSD_CHUNK) == 0 and rows % NA_Q_ROWS == 0
    assert rows >= NA_K_ROWS and t % S5_CHUNK == 0
    row1 = lambda v: v.reshape(1, -1).astype(F32)
    mem2 = mem.reshape(-1, D_MODEL)
    h = x.reshape(n, D_MODEL)
    for l in range(DEPTH):
        dt_bias = jnp.pad(row1(ssd_dt_bias[l]), ((0, 0), (0, DT_PAD - 2 * SSD_HEADS)))
        outs = _inproj(h, row1(ln_in_g), row1(ln_in_b), _permute_w_in(w_in[l]), ssd_conv_w[l].astype(F32),
                       row1(ssd_conv_b[l]), dt_bias, t, apply_ln=(l == 0))
        qkv, z, xc, bt, u_cm, dtt = outs[:6]
        if l == 0:
            h = outs[6]
        o_na = _na(qkv, _na_bias_tables(na_rpb[l].astype(F32), rows), bsz, t)

        xc3 = xc.reshape(bsz, t, SSD_CONV_DIM)
        a_log16 = row1(ssd_a_log[l])
        prev_f, prev_b = _ssd_states(xc3, bt, dtt, a_log16)
        o_ssd = _ssd_out(xc3, z.reshape(bsz, t, SSD_DIM), dtt, prev_f, prev_b, a_log16,
                         row1(jnp.repeat(ssd_d[l], HEAD_DIM)), row1(ssd_norm_w[l]))

        tables = _s5_tables(s5_lam_re[l].astype(F32), s5_lam_im[l].astype(F32), s5_log_dt[l].astype(F32),
                            s5_b_re[l].astype(F32), s5_b_im[l].astype(F32), s5_c_re[l].astype(F32),
                            s5_c_im[l].astype(F32))
        y5 = _s5(u_cm, tables, row1(s5_d[l]), bsz, t)

        h = _mixout(h, o_na, o_ssd, y5, s5_glu_w[l].astype(BF16), row1(s5_glu_b[l]),
                    w_mix_out[l].astype(BF16), row1(ln_mix_g[l]), row1(ln_mix_b[l]))
        k2, v2 = _kvproj(mem2, xa_wk[l].astype(BF16), xa_wv[l].astype(BF16))
        h = _xattn(h, k2.reshape(bsz, -1, D_MODEL), v2.reshape(bsz, -1, D_MODEL), xa_wq[l].astype(BF16),
                   xa_wo[l].astype(BF16), row1(ln_xa_g[l]), row1(ln_xa_b[l]), bsz, t)
        h = _mlp(h, mlp_w1[l].astype(BF16), mlp_w2[l].astype(BF16), row1(ln_mlp_g[l]), row1(ln_mlp_b[l]))
    return h.reshape(bsz, t, D_MODEL)
```

```python
import functools
import math

import jax
import jax.numpy as jnp
import numpy as np
from jax import lax
from jax.experimental import pallas as pl
from jax.experimental.pallas import tpu as pltpu

F32 = jnp.float32
BF16 = jnp.bfloat16
HIGHEST = lax.Precision.HIGHEST

D_MODEL = 1024
DEPTH = 2
GRID_W = 64
HEAD_DIM = 64
NA_DIM = 256
NA_HEADS = 4
NA_WIN_ROWS = 8
NA_WIN_COLS = 16
SSD_DIM = 512
SSD_HEADS = 8
SSD_GROUPS = 2
SSD_STATE = 64
SSD_CONV = 5
SSD_CHUNK = 128
SSD_CONV_DIM = SSD_DIM + 2 * SSD_GROUPS * SSD_STATE
S5_DIM = 256
S5_GROUP_CH = 16
S5_GROUPS = 16
S5_STATE = 64
XA_HEADS = 4
XA_HEAD_DIM = 256
D_FF = 4096
LN_EPS = 1e-5
NEG_BIG = -1e30
DEEPNORM_ALPHA = (2 * DEPTH) ** 0.25

ROW_TILE = 512
XA_ROW_TILE = 2048
MIX_ROW_TILE = 1024
MLP_ROW_TILE = 1024
IN_ROW_TILE = 1024
NA_Q_ROWS = 4
NA_K_ROWS = NA_Q_ROWS + NA_WIN_ROWS
NA_BLOCKS_PER_STEP = 4
SSD_BLOCK_CHUNKS = 16
S5_CHUNK = 16
LANES = 128
DT_PAD = LANES
VMEM_LIMIT = 56 * 1024 * 1024


def _params(*sem):
    return pltpu.CompilerParams(dimension_semantics=sem, vmem_limit_bytes=VMEM_LIMIT)


def _layernorm(x, g, b):
    mu = jnp.mean(x, axis=-1, keepdims=True)
    xc = x - mu
    var = jnp.mean(xc * xc, axis=-1, keepdims=True)
    return xc * lax.rsqrt(var + LN_EPS) * g + b


def _sigmoid(x):
    return 0.5 + 0.5 * jnp.tanh(0.5 * x)


def _silu(x):
    return x * _sigmoid(x)


def _softplus(x):
    return jnp.maximum(x, 0.0) + jnp.log1p(jnp.exp(-jnp.abs(x)))


def _const_spec(shape):
    n = len(shape)
    return pl.BlockSpec(shape, lambda *_: (0,) * n)


IN_QKV, IN_Z, IN_XBC, IN_U, IN_DT = (0, 768), (768, 1280), (1280, 2048), (2048, 2304), (2304, 2304 + DT_PAD)
PROJ_HALO = 16


def _inproj_kernel(x_ref, xp_ref, xn_ref, g_ref, b_ref, w_ref, cw_ref, cb_ref, dtb_ref,
                   qkv_ref, z_ref, xc_ref, bt_ref, u_ref, dtt_ref, *rest, apply_ln, tiles_per_seq):
    xe_ref, us_ref = rest[-2:]
    tm, hl = IN_ROW_TILE, PROJ_HALO
    pos = pl.program_id(0) % tiles_per_seq
    x, xp, xn = x_ref[...], xp_ref[...], xn_ref[...]
    if apply_ln:
        x = _layernorm(x, g_ref[...], b_ref[...])
        xp = _layernorm(xp, g_ref[...], b_ref[...])
        xn = _layernorm(xn, g_ref[...], b_ref[...])
        rest[0][...] = x
    xe_ref[0:hl, :] = xp.astype(BF16)
    xe_ref[hl:hl + tm, :] = x.astype(BF16)
    xe_ref[hl + tm:, :] = xn.astype(BF16)

    half = tm // 2
    pad = SSD_CONV // 2
    n_slab = S5_DIM // LANES
    for r in range(2):
        rs = slice(r * half, (r + 1) * half)
        xbc = jnp.dot(xe_ref[r * half:r * half + half + 2 * hl, :], w_ref[:, IN_XBC[0]:IN_XBC[1]],
                      preferred_element_type=F32)
        before = jnp.where(pos == 0, 0.0, xbc[0:hl]) if r == 0 else xbc[0:hl]
        after = jnp.where(pos == tiles_per_seq - 1, 0.0, xbc[hl + half:]) if r == 1 else xbc[hl + half:]
        xbc = jnp.concatenate([before, xbc[hl:hl + half], after], axis=0)

        xb = xe_ref[hl + r * half:hl + (r + 1) * half, :]

        def proj(cols):
            return jnp.dot(xb, w_ref[:, cols[0]:cols[1]], preferred_element_type=F32)

        qkv_ref[rs, :] = proj(IN_QKV).astype(qkv_ref.dtype)
        z_ref[rs, :] = proj(IN_Z)
        u = proj(IN_U)
        dt = _softplus(proj(IN_DT) + dtb_ref[...])
        dtt_ref[:, rs] = dt.T[:2 * SSD_HEADS, :]

        acc = cb_ref[...]
        ext = half + 2 * hl
        for k in range(SSD_CONV):
            shifted = xbc if k == pad else pltpu.roll(xbc, (pad - k) % ext, 0)
            acc = acc + shifted[hl:hl + half] * cw_ref[k:k + 1, :]
        xc = _silu(acc)
        xc_ref[rs, :] = xc.astype(xc_ref.dtype)
        bt_ref[:, rs] = xc[:, SSD_DIM:SSD_DIM + SSD_GROUPS * SSD_STATE].T.astype(bt_ref.dtype)

        cpr = half // S5_CHUNK
        for k in range(n_slab):
            us_ref[r, k] = u[:, k * LANES:(k + 1) * LANES]
        for s in range(S5_CHUNK):
            for k in range(n_slab):
                u_ref[r * cpr:(r + 1) * cpr, s * S5_DIM + k * LANES:s * S5_DIM + (k + 1) * LANES] = \
                    us_ref[r, k, pl.ds(s, cpr, stride=S5_CHUNK), :]


def _inproj(x, g, b, w, conv_w, conv_b, dt_bias, t, apply_ln):
    n = x.shape[0]
    tm, hl = IN_ROW_TILE, PROJ_HALO
    per = tm // hl
    row = lambda wd: pl.BlockSpec((tm, wd), lambda i: (i, 0))
    bn = SSD_GROUPS * SSD_STATE
    out_shape = [jax.ShapeDtypeStruct((n, 3 * NA_DIM), BF16), jax.ShapeDtypeStruct((n, SSD_DIM), F32),
                 jax.ShapeDtypeStruct((n, SSD_CONV_DIM), BF16), jax.ShapeDtypeStruct((bn, n), BF16),
                 jax.ShapeDtypeStruct((n // S5_CHUNK, S5_CHUNK * S5_DIM), F32),
                 jax.ShapeDtypeStruct((2 * SSD_HEADS, n), F32)]
    out_specs = [row(3 * NA_DIM), row(SSD_DIM), row(SSD_CONV_DIM), pl.BlockSpec((bn, tm), lambda i: (0, i)),
                 pl.BlockSpec((tm // S5_CHUNK, S5_CHUNK * S5_DIM), lambda i: (i, 0)),
                 pl.BlockSpec((2 * SSD_HEADS, tm), lambda i: (0, i))]
    if apply_ln:
        out_shape.append(jax.ShapeDtypeStruct((n, D_MODEL), F32))
        out_specs.append(row(D_MODEL))
    return pl.pallas_call(
        functools.partial(_inproj_kernel, apply_ln=apply_ln, tiles_per_seq=t // tm),
        grid=(n // tm,),
        in_specs=[row(D_MODEL),
                  pl.BlockSpec((hl, D_MODEL), lambda i: (jnp.maximum(i * per - 1, 0), 0)),
                  pl.BlockSpec((hl, D_MODEL), lambda i: (jnp.minimum((i + 1) * per, n // hl - 1), 0)),
                  _const_spec((1, D_MODEL)), _const_spec((1, D_MODEL)), _const_spec(w.shape),
                  _const_spec((SSD_CONV, SSD_CONV_DIM)), _const_spec((1, SSD_CONV_DIM)), _const_spec((1, DT_PAD))],
        out_specs=out_specs, out_shape=out_shape,
        scratch_shapes=[pltpu.VMEM((tm + 2 * hl, D_MODEL), BF16),
                        pltpu.VMEM((2, S5_DIM // LANES, tm // 2, LANES), F32)],
        compiler_params=_params("parallel"), name="in_proj",
    )(x, x, x, g, b, w, conv_w, conv_b, dt_bias)


def _na_bias_tables(rpb, rows):
    n_ri, n_ci = 2 * NA_WIN_ROWS - 1, 2 * NA_WIN_COLS - 1
    qr = np.arange(NA_Q_ROWS)
    kr = np.arange(NA_K_ROWS)
    c = np.arange(GRID_W)
    c0 = np.clip(c - NA_WIN_COLS // 2, 0, GRID_W - NA_WIN_COLS)
    col_ok = (c[None, :] >= c0[:, None]) & (c[None, :] < c0[:, None] + NA_WIN_COLS)
    ci = np.clip(c[None, :] - c[:, None], -(NA_WIN_COLS - 1), NA_WIN_COLS - 1) + (NA_WIN_COLS - 1)
    onehot_ci = (ci[None] == np.arange(n_ci)[:, None, None]).astype(np.float32)
    onehot_ri, ok = [], []
    for blk_row in (0, NA_Q_ROWS, rows - NA_Q_ROWS):
        start = min(max(blk_row - NA_WIN_ROWS // 2, 0), rows - NA_K_ROWS)
        r = blk_row + qr
        r0 = np.clip(r - NA_WIN_ROWS // 2, 0, rows - NA_WIN_ROWS)
        key_row = start + kr
        row_ok = (key_row[None, :] >= r0[:, None]) & (key_row[None, :] < r0[:, None] + NA_WIN_ROWS)
        ri = key_row[None, :] - r[:, None] + (NA_WIN_ROWS - 1)
        onehot_ri.append(((ri[..., None] == np.arange(n_ri)) & row_ok[..., None]).astype(np.float32))
        ok.append(row_ok[:, None, :, None] & col_ok[None, :, None, :])
    col_tab = jnp.einsum('hrc,cqk->hrqk', rpb, jnp.asarray(onehot_ci), precision=HIGHEST)
    bias = jnp.einsum('aqjr,hrwk->ahqwjk', jnp.asarray(np.stack(onehot_ri)), col_tab, precision=HIGHEST)
    bias = jnp.where(jnp.asarray(np.stack(ok))[:, None], bias, NEG_BIG)
    return bias.reshape(3, NA_HEADS, NA_Q_ROWS * GRID_W, NA_K_ROWS * GRID_W).astype(F32)


def _na_kernel(q_ref, k_ref, v_ref, bias_ref, o_ref, *, rows):
    nblk = rows // NA_Q_ROWS
    nk = NA_K_ROWS * GRID_W
    tq = NA_Q_ROWS * GRID_W
    for r in range(NA_BLOCKS_PER_STEP):
        i = pl.program_id(1) * NA_BLOCKS_PER_STEP + r
        case = jnp.where(i == 0, 0, jnp.where(i == nblk - 1, 2, 1))
        start_row = jnp.clip(i * NA_Q_ROWS - NA_WIN_ROWS // 2, 0, rows - NA_K_ROWS)
        start = pl.multiple_of(start_row * GRID_W, GRID_W)
        rs = slice(r * tq, (r + 1) * tq)
        q = q_ref[rs, :] * (HEAD_DIM ** -0.5)
        kw = k_ref[pl.ds(start, nk), :]
        vw = v_ref[pl.ds(start, nk), :]
        for h in range(NA_HEADS):
            sl = slice(h * HEAD_DIM, (h + 1) * HEAD_DIM)
            s = lax.dot_general(q[:, sl], kw[:, sl], (((1,), (1,)), ((), ())), preferred_element_type=F32)
            s = s + bias_ref[case, h]
            m = jnp.max(s, axis=-1, keepdims=True)
            p = jnp.exp(s - m)
            l = jnp.sum(p, axis=-1, keepdims=True)
            o = jnp.dot(p.astype(BF16), vw[:, sl], preferred_element_type=F32)
            o_ref[rs, sl] = (o / l).astype(o_ref.dtype)


def _na(qkv, bias, bsz, t):
    rows = t // GRID_W
    tq = NA_BLOCKS_PER_STEP * NA_Q_ROWS * GRID_W
    qkv3 = qkv.reshape(bsz, t, 3 * NA_DIM)
    out = pl.pallas_call(
        functools.partial(_na_kernel, rows=rows),
        grid=(bsz, t // tq),
        in_specs=[pl.BlockSpec((None, tq, NA_DIM), lambda b, i: (b, i, 0)),
                  pl.BlockSpec((None, t, NA_DIM), lambda b, i: (b, 0, 1)),
                  pl.BlockSpec((None, t, NA_DIM), lambda b, i: (b, 0, 2)),
                  pl.BlockSpec(bias.shape, lambda b, i: (0, 0, 0, 0), pipeline_mode=pl.Buffered(1))],
        out_specs=pl.BlockSpec((None, tq, NA_DIM), lambda b, i: (b, i, 0)),
        out_shape=jax.ShapeDtypeStruct((bsz, t, NA_DIM), BF16),
        compiler_params=_params("parallel", "arbitrary"), name="na_attn",
    )(qkv3, qkv3, qkv3, bias)
    return out.reshape(bsz * t, NA_DIM)


def _tri(n, lower):
    r = lax.broadcasted_iota(jnp.int32, (n, n), 0)
    c = lax.broadcasted_iota(jnp.int32, (n, n), 1)
    return (r >= c) if lower else (r <= c)


def _split3(x):
    hi = x.astype(BF16).astype(F32)
    r = x - hi
    mid = r.astype(BF16).astype(F32)
    lo = (r - mid).astype(BF16).astype(F32)
    return hi, mid, lo


def _dot01_left(m01, x):
    return sum(jnp.dot(m01, p.astype(BF16), preferred_element_type=F32) for p in _split3(x))


def _dot01_right(x, m01):
    return sum(jnp.dot(p.astype(BF16), m01, preferred_element_type=F32) for p in _split3(x))


def _head_expand():
    r = lax.broadcasted_iota(jnp.int32, (SSD_HEADS, SSD_DIM), 0)
    c = lax.broadcasted_iota(jnp.int32, (SSD_HEADS, SSD_DIM), 1)
    return (c // HEAD_DIM == r).astype(BF16)


def _ssd_state_kernel(xf_ref, xb_ref, btf_ref, btb_ref, dttf_ref, dttb_ref, alog_ref,
                      pf_ref, pb_ref, sf_ref, sb_ref):
    q = SSD_CHUNK
    nh = SSD_HEADS
    pair_w = 2 * HEAD_DIM

    @pl.when(pl.program_id(1) == 0)
    def _():
        sf_ref[...] = jnp.zeros_like(sf_ref)
        sb_ref[...] = jnp.zeros_like(sb_ref)

    row_id = lax.broadcasted_iota(jnp.int32, (q, q), 0)
    col_id = lax.broadcasted_iota(jnp.int32, (q, q), 1)
    first_half = lax.broadcasted_iota(jnp.int32, (SSD_STATE, pair_w), 1) < HEAD_DIM
    a_col = -jnp.exp(alog_ref[...])
    for d, (x_ref, bt_ref, dtt_ref, prev_ref, state_ref) in enumerate(
            ((xf_ref, btf_ref, dttf_ref, pf_ref, sf_ref), (xb_ref, btb_ref, dttb_ref, pb_ref, sb_ref))):
        backward = d == 1
        hs = slice(d * nh, (d + 1) * nh)
        tri = ((row_id >= col_id) if backward else (row_id <= col_id)).astype(BF16)
        dtr = dtt_ref[hs, :]
        da = dtr * a_col[hs]
        local, decay = [], []
        da_rows = jnp.concatenate([da[:, cc * q:(cc + 1) * q] for cc in range(SSD_BLOCK_CHUNKS)], axis=0)
        cs_rows = _dot01_right(da_rows, tri)
        for cc in range(SSD_BLOCK_CHUNKS):
            rs = slice(cc * q, (cc + 1) * q)
            cs = cs_rows[cc * nh:(cc + 1) * nh]
            tot = cs[:, 0:1] if backward else cs[:, q - 1:q]
            w = jnp.exp(tot - cs) * dtr[:, rs]
            chunk_decay = jnp.exp(tot)
            local.append([])
            decay.append([])
            for pp in range(nh // 2):
                g = (2 * pp) // (nh // SSD_GROUPS)
                bt = bt_ref[g * SSD_STATE:(g + 1) * SSD_STATE, rs].astype(F32)
                xs = x_ref[rs, pp * pair_w:(pp + 1) * pair_w]
                lhs = jnp.concatenate([(bt * w[h:h + 1, :]).astype(BF16) for h in (2 * pp, 2 * pp + 1)], axis=0)
                both = jnp.dot(lhs, xs, preferred_element_type=F32)
                local[cc].append(jnp.where(first_half, both[:SSD_STATE], both[SSD_STATE:]))
                decay[cc].append(jnp.where(first_half, chunk_decay[2 * pp:2 * pp + 1, :],
                                           chunk_decay[2 * pp + 1:2 * pp + 2, :]))
        order = range(SSD_BLOCK_CHUNKS - 1, -1, -1) if backward else range(SSD_BLOCK_CHUNKS)
        for pp in range(nh // 2):
            lanes = slice(pp * pair_w, (pp + 1) * pair_w)
            state = state_ref[:, lanes]
            for cc in order:
                prev_ref[cc, :, lanes] = state.astype(prev_ref.dtype)
                state = state * decay[cc][pp] + local[cc][pp]
            state_ref[:, lanes] = state


def _ssd_states(xc3, bt, dtt, a_log16):
    bsz, t, _ = xc3.shape
    tb = SSD_BLOCK_CHUNKS * SSD_CHUNK
    nblk = t // tb
    bn = SSD_GROUPS * SSD_STATE
    fwd_rows = pl.BlockSpec((None, tb, SSD_CONV_DIM), lambda b, j: (b, j, 0))
    bwd_rows = pl.BlockSpec((None, tb, SSD_CONV_DIM), lambda b, j: (b, nblk - 1 - j, 0))
    fwd_cols = lambda rows: pl.BlockSpec((rows, tb), lambda b, j: (0, b * nblk + j))
    bwd_cols = lambda rows: pl.BlockSpec((rows, tb), lambda b, j: (0, b * nblk + nblk - 1 - j))
    state = jax.ShapeDtypeStruct((bsz, t // SSD_CHUNK, SSD_STATE, SSD_DIM), BF16)
    return pl.pallas_call(
        _ssd_state_kernel, grid=(bsz, nblk),
        in_specs=[fwd_rows, bwd_rows, fwd_cols(bn), bwd_cols(bn), fwd_cols(2 * SSD_HEADS), bwd_cols(2 * SSD_HEADS),
                  _const_spec((2 * SSD_HEADS, 1))],
        out_specs=[pl.BlockSpec((None, SSD_BLOCK_CHUNKS, SSD_STATE, SSD_DIM), lambda b, j: (b, j, 0, 0)),
                   pl.BlockSpec((None, SSD_BLOCK_CHUNKS, SSD_STATE, SSD_DIM),
                                lambda b, j: (b, nblk - 1 - j, 0, 0))],
        out_shape=[state, state],
        scratch_shapes=[pltpu.VMEM((SSD_STATE, SSD_DIM), F32), pltpu.VMEM((SSD_STATE, SSD_DIM), F32)],
        compiler_params=_params("parallel", "arbitrary"), name="ssd_state",
    )(xc3, xc3, bt, bt, dtt, dtt, a_log16.reshape(2 * SSD_HEADS, 1))


def _ssd_out_kernel(x_ref, z_ref, dtt_ref, pf_ref, pb_ref, alog_col_ref, dskip_ref, nw_ref, o_ref):
    q = SSD_CHUNK
    nh = SSD_HEADS
    row_id = lax.broadcasted_iota(jnp.int32, (q, q), 0)
    col_id = lax.broadcasted_iota(jnp.int32, (q, q), 1)
    lower = row_id >= col_id
    eye = row_id == col_id
    first_half = lax.broadcasted_iota(jnp.int32, (q, 2 * HEAD_DIM), 1) < HEAD_DIM
    tri_l = lower.astype(BF16)
    tri_u = (row_id <= col_id).astype(BF16)
    a_col = -jnp.exp(alog_col_ref[...])
    dtt_all = dtt_ref[...]
    da = dtt_all * a_col
    da_rows = jnp.concatenate([da[:, cc * q:(cc + 1) * q] for cc in range(SSD_BLOCK_CHUNKS)], axis=0)
    cs_rows_f, cs_rows_b = _dot01_right(da_rows, tri_u), _dot01_right(da_rows, tri_l)
    cs_cols_f, cs_cols_b = cs_rows_f.T, cs_rows_b.T
    e_cols_f, e_cols_b = jnp.exp(cs_cols_f), jnp.exp(cs_cols_b)
    log_dt = jnp.log(dtt_all)
    ldt_rows = jnp.concatenate([log_dt[:, cc * q:(cc + 1) * q] for cc in range(SSD_BLOCK_CHUNKS)], axis=0)
    rp_rows_f, rp_rows_b = cs_rows_f - ldt_rows, cs_rows_b - ldt_rows
    for cc in range(SSD_BLOCK_CHUNKS):
        rs = slice(cc * q, (cc + 1) * q)
        dtr = dtt_all[:, rs]
        hrow = slice(cc * 2 * nh, (cc + 1) * 2 * nh)
        cs_row_f, cs_row_b = rp_rows_f[hrow], rp_rows_b[hrow]
        cs_col_f, cs_col_b = cs_cols_f[:, hrow], cs_cols_b[:, hrow]
        e_col_f, e_col_b = e_cols_f[:, hrow], e_cols_b[:, hrow]
        bc = x_ref[rs, SSD_DIM:SSD_CONV_DIM]
        groups = []
        for g in range(SSD_GROUPS):
            bg = bc[:, g * SSD_STATE:(g + 1) * SSD_STATE]
            cg = bc[:, (SSD_GROUPS + g) * SSD_STATE:(SSD_GROUPS + g + 1) * SSD_STATE]
            cb = lax.dot_general(cg, bg, (((1,), (1,)), ((), ())), preferred_element_type=F32)
            c2 = jnp.concatenate([cg, cg], axis=1).astype(F32)
            pairs = []
            for pp in range(SSD_HEADS // SSD_GROUPS // 2):
                h0 = g * (SSD_HEADS // SSD_GROUPS) + 2 * pp
                lanes = slice(h0 * HEAD_DIM, (h0 + 2) * HEAD_DIM)
                rhs = jnp.concatenate([x_ref[rs, lanes], pf_ref[cc, :, lanes], pb_ref[cc, :, lanes]], axis=0)
                lhs = []
                for h in (h0, h0 + 1):
                    seg_f = cs_col_f[:, h:h + 1] - cs_row_f[h:h + 1, :]
                    seg_b = cs_col_b[:, nh + h:nh + h + 1] - cs_row_b[nh + h:nh + h + 1, :]
                    lmat = (jnp.exp(jnp.where(lower, seg_f, seg_b))
                            + jnp.where(eye, dtr[nh + h:nh + h + 1, :], 0.0))
                    e2 = jnp.where(first_half, e_col_f[:, h:h + 1], e_col_b[:, nh + h:nh + h + 1])
                    lhs.append(jnp.concatenate([(cb * lmat).astype(BF16), (c2 * e2).astype(BF16)], axis=1))
                both = jnp.dot(jnp.concatenate(lhs, axis=0), rhs, preferred_element_type=F32)
                pairs.append(jnp.where(first_half, both[:q], both[q:]))
            groups.append(jnp.concatenate(pairs, axis=1))
        y = jnp.concatenate(groups, axis=1) + dskip_ref[...] * x_ref[rs, :SSD_DIM].astype(F32)
        yg = y * _silu(z_ref[rs, :])
        ms = jnp.mean(yg * yg, axis=-1, keepdims=True)
        o_ref[rs, :] = (yg * lax.rsqrt(ms + LN_EPS) * nw_ref[...]).astype(o_ref.dtype)


def _ssd_out(xc3, z3, dtt, prev_f, prev_b, a_log16, d_skip, norm_w):
    bsz, t, _ = xc3.shape
    tb = SSD_BLOCK_CHUNKS * SSD_CHUNK
    nblk = t // tb
    blk = lambda wd: pl.BlockSpec((None, tb, wd), lambda b, j: (b, j, 0))
    prev = pl.BlockSpec((None, SSD_BLOCK_CHUNKS, SSD_STATE, SSD_DIM), lambda b, j: (b, j, 0, 0))
    out = pl.pallas_call(
        _ssd_out_kernel, grid=(bsz, nblk),
        in_specs=[blk(SSD_CONV_DIM), blk(SSD_DIM),
                  pl.BlockSpec((2 * SSD_HEADS, tb), lambda b, j: (0, b * nblk + j)),
                  prev, prev, _const_spec((2 * SSD_HEADS, 1)),
                  _const_spec((1, SSD_DIM)), _const_spec((1, SSD_DIM))],
        out_specs=blk(SSD_DIM),
        out_shape=jax.ShapeDtypeStruct((bsz, t, SSD_DIM), BF16),
        compiler_params=_params("parallel", "parallel"), name="ssd_out",
    )(xc3, z3, dtt, prev_f, prev_b, a_log16.reshape(2 * SSD_HEADS, 1), d_skip, norm_w)
    return out.reshape(bsz * t, SSD_DIM)


def _s5_tables(lam_re, lam_im, log_dt, b_re, b_im, c_re, c_im):
    lc = S5_CHUNK
    hp = HIGHEST
    pw_re, pw_im, bb_re, bb_im = [], [], [], []
    for d in range(2):
        dt = jnp.exp(log_dt[d])[:, None]
        lr, li = lam_re[d], lam_im[d]
        mag = jnp.exp(lr * dt)
        ar, ai = mag * jnp.cos(li * dt), mag * jnp.sin(li * dt)
        den = lr * lr + li * li
        fr = ((ar - 1.0) * lr + ai * li) / den
        fi = (ai * lr - (ar - 1.0) * li) / den
        bb_re.append(fr[..., None] * b_re[d] - fi[..., None] * b_im[d])
        bb_im.append(fr[..., None] * b_im[d] + fi[..., None] * b_re[d])
        k = jnp.arange(lc + 1, dtype=F32)[:, None, None]
        pw_re.append(jnp.exp(k * (lr * dt)) * jnp.cos(k * (li * dt)))
        pw_im.append(jnp.exp(k * (lr * dt)) * jnp.sin(k * (li * dt)))

    def kern(d):
        wr = pw_re[d][:lc, :, :, None] * bb_re[d][None] - pw_im[d][:lc, :, :, None] * bb_im[d][None]
        wi = pw_re[d][:lc, :, :, None] * bb_im[d][None] + pw_im[d][:lc, :, :, None] * bb_re[d][None]
        return (jnp.einsum('ghp,kgpj->kghj', c_re[d], wr, precision=hp)
                - jnp.einsum('ghp,kgpj->kghj', c_im[d], wi, precision=hp))

    kf, kb = kern(0), kern(1)
    l = jnp.arange(lc)
    lag = l[:, None] - l[None, :]
    tf = jnp.where((lag >= 0)[:, :, None, None, None], kf[jnp.clip(lag, 0, lc - 1)], 0.0)
    tb = jnp.where((lag <= 0)[:, :, None, None, None], kb[jnp.clip(-lag, 0, lc - 1)], 0.0)
    toep = (tf + tb).transpose(2, 1, 4, 0, 3).reshape(S5_GROUPS, lc * S5_GROUP_CH, lc * S5_GROUP_CH)

    def state_in(d, powers):
        wr = pw_re[d][powers][:, :, :, None] * bb_re[d][None] - pw_im[d][powers][:, :, :, None] * bb_im[d][None]
        wi = pw_re[d][powers][:, :, :, None] * bb_im[d][None] + pw_im[d][powers][:, :, :, None] * bb_re[d][None]
        to_rows = lambda w: w.transpose(1, 0, 3, 2).reshape(S5_GROUPS, lc * S5_GROUP_CH, S5_STATE)
        return to_rows(wr), to_rows(wi)

    f_re, f_im = state_in(0, lc - 1 - l)
    b_re_, b_im_ = state_in(1, l)
    m_state = jnp.concatenate([f_re, b_re_, f_im, b_im_], axis=-1)

    def state_out(d, powers):
        cpr = c_re[d][None] * pw_re[d][powers][:, :, None, :] - c_im[d][None] * pw_im[d][powers][:, :, None, :]
        cpi = c_re[d][None] * pw_im[d][powers][:, :, None, :] + c_im[d][None] * pw_re[d][powers][:, :, None, :]
        to_cols = lambda w: w.transpose(1, 3, 0, 2).reshape(S5_GROUPS, S5_STATE, lc * S5_GROUP_CH)
        return to_cols(cpr), -to_cols(cpi)

    of_re, of_im = state_out(0, l + 1)
    ob_re, ob_im = state_out(1, lc - l)
    m_off = jnp.concatenate([of_re, ob_re, of_im, ob_im], axis=1)
    dec_re = jnp.concatenate([pw_re[0][lc], pw_re[1][lc]], axis=-1)[:, None, :]
    dec_im = jnp.concatenate([pw_im[0][lc], pw_im[1][lc]], axis=-1)[:, None, :]
    return toep.astype(BF16), m_state.astype(BF16), m_off.astype(BF16), dec_re, dec_im


S5_W = S5_CHUNK * S5_GROUP_CH
S5_RELAYOUT_CHUNKS = 64
S5_GROUPS_PER_STEP = 2


def _s5_group_kernel(u_ref, o_ref):
    x = u_ref[...]
    for g in range(S5_GROUPS):
        lo = g * S5_GROUP_CH
        pieces = [x[:, s * S5_DIM + lo:s * S5_DIM + lo + S5_GROUP_CH] for s in range(S5_CHUNK)]
        o_ref[g] = jnp.concatenate(pieces, axis=-1).astype(o_ref.dtype)


def _s5_ungroup_kernel(y_ref, u_ref, d_ref, o_ref):
    ys = [y_ref[g] for g in range(S5_GROUPS)]
    for l in range(S5_CHUNK):
        cols = slice(l * S5_DIM, (l + 1) * S5_DIM)
        pieces = [y[:, l * S5_GROUP_CH:(l + 1) * S5_GROUP_CH] for y in ys]
        o_ref[:, cols] = jnp.concatenate(pieces, axis=-1) + d_ref[...] * u_ref[:, cols]


def _s5_kernel(u_ref, toep_ref, mst_ref, moff_ref, are_ref, aim_ref, y_ref, s_ref, e_ref, *, nc, bsz):
    ng = S5_GROUPS_PER_STEP
    half = S5_STATE
    w = S5_W
    for gi in range(ng):
        for b in range(bsz):
            ub = u_ref[gi, :, b * w:(b + 1) * w]
            y_ref[gi, :, b * w:(b + 1) * w] = jnp.dot(ub, toep_ref[gi], preferred_element_type=F32)
            s = jnp.dot(ub, mst_ref[gi], preferred_element_type=F32)
            for k in range(2):
                s_ref[gi, k, pl.ds(b, nc, stride=bsz), :] = s[:, k * LANES:(k + 1) * LANES]
    ar = [jnp.broadcast_to(are_ref[gi], (bsz, 2 * half)) for gi in range(ng)]
    ai = [jnp.broadcast_to(aim_ref[gi], (bsz, 2 * half)) for gi in range(ng)]
    is_fwd = lax.broadcasted_iota(jnp.int32, (bsz, 2 * half), 1) < half

    def body(i, carry):
        rf = pl.multiple_of(i * bsz, bsz)
        rb = pl.multiple_of((nc - 1 - i) * bsz, bsz)
        out = []
        for gi in range(ng):
            er, ei = carry[gi]
            e_ref[gi, 0, pl.ds(rf, bsz), 0:half] = er[:, :half]
            e_ref[gi, 0, pl.ds(rb, bsz), half:] = er[:, half:]
            e_ref[gi, 1, pl.ds(rf, bsz), 0:half] = ei[:, :half]
            e_ref[gi, 1, pl.ds(rb, bsz), half:] = ei[:, half:]
            sr = jnp.where(is_fwd, s_ref[gi, 0, pl.ds(rf, bsz), :], s_ref[gi, 0, pl.ds(rb, bsz), :])
            si = jnp.where(is_fwd, s_ref[gi, 1, pl.ds(rf, bsz), :], s_ref[gi, 1, pl.ds(rb, bsz), :])
            out.append((ar[gi] * er - ai[gi] * ei + sr, ar[gi] * ei + ai[gi] * er + si))
        return tuple(out)

    zero = jnp.zeros((bsz, 2 * half), F32)
    lax.fori_loop(0, nc, body, tuple((zero, zero) for _ in range(ng)), unroll=2)
    for gi in range(ng):
        for b in range(bsz):
            e = jnp.concatenate([e_ref[gi, k, pl.ds(b, nc, stride=bsz), :] for k in range(2)], axis=-1)
            y_ref[gi, :, b * w:(b + 1) * w] += jnp.dot(e.astype(BF16), moff_ref[gi], preferred_element_type=F32)


def _s5(u_cm, tables, d_skip, bsz, t):
    toep, m_state, m_off, dec_re, dec_im = tables
    nc = t // S5_CHUNK
    w = S5_W
    rc = min(S5_RELAYOUT_CHUNKS, nc)
    u3 = u_cm.reshape(bsz, nc, S5_CHUNK * S5_DIM)
    tok = pl.BlockSpec((None, rc, S5_CHUNK * S5_DIM), lambda b, j: (b, j, 0))
    grouped = pl.BlockSpec((S5_GROUPS, rc, w), lambda b, j: (0, j, b))
    ug = pl.pallas_call(
        _s5_group_kernel, grid=(bsz, nc // rc), in_specs=[tok], out_specs=grouped,
        out_shape=jax.ShapeDtypeStruct((S5_GROUPS, nc, bsz * w), BF16),
        compiler_params=_params("parallel", "parallel"), name="s5_group",
    )(u3)
    ng = S5_GROUPS_PER_STEP
    grp = lambda shape: pl.BlockSpec((ng,) + shape, lambda g: (g, 0, 0))
    yg = pl.pallas_call(
        functools.partial(_s5_kernel, nc=nc, bsz=bsz),
        grid=(S5_GROUPS // ng,),
        in_specs=[grp((nc, bsz * w)), grp((w, w)), grp((w, w)), grp((w, w)),
                  grp((1, 2 * S5_STATE)), grp((1, 2 * S5_STATE))],
        out_specs=grp((nc, bsz * w)),
        out_shape=jax.ShapeDtypeStruct((S5_GROUPS, nc, bsz * w), F32),
        scratch_shapes=[pltpu.VMEM((ng, 2, nc * bsz, LANES), F32), pltpu.VMEM((ng, 2, nc * bsz, LANES), F32)],
        compiler_params=_params("parallel"), name="s5_scan",
    )(ug, toep, m_state, m_off, dec_re, dec_im)
    y = pl.pallas_call(
        _s5_ungroup_kernel, grid=(bsz, nc // rc),
        in_specs=[grouped, tok, pl.BlockSpec((1, S5_DIM), lambda b, j: (0, 0))], out_specs=tok,
        out_shape=jax.ShapeDtypeStruct((bsz, nc, S5_CHUNK * S5_DIM), F32),
        compiler_params=_params("parallel", "parallel"), name="s5_ungroup",
    )(yg, u3, d_skip)
    return y.reshape(bsz * nc, S5_CHUNK * S5_DIM)


def _mixout_kernel(h_ref, na_ref, ssd_ref, y5_ref, gw_ref, gb_ref, wo_ref, g_ref, b_ref, o_ref, ys_ref):
    tm = h_ref.shape[0]
    n_slab = S5_DIM // LANES
    for l in range(S5_CHUNK):
        for k in range(n_slab):
            ys_ref[k, pl.ds(l, tm // S5_CHUNK, stride=S5_CHUNK), :] = \
                y5_ref[:, l * S5_DIM + k * LANES:l * S5_DIM + (k + 1) * LANES]
    half = tm // 2
    for r in range(2):
        rs = slice(r * half, (r + 1) * half)
        y5 = jnp.concatenate([ys_ref[k, rs, :] for k in range(n_slab)], axis=-1)
        gl = 0.5 * y5 * (1.0 + jnp.tanh(math.sqrt(2.0 / math.pi) * (y5 + 0.044715 * (y5 * y5 * y5))))
        gate = jnp.dot(gl.astype(BF16), gw_ref[...], preferred_element_type=F32) + gb_ref[...]
        o5 = gl * _sigmoid(gate)
        mixed = jnp.concatenate([na_ref[rs, :], ssd_ref[rs, :], o5.astype(BF16)], axis=-1)
        mix = jnp.dot(mixed, wo_ref[...], preferred_element_type=F32)
        o_ref[rs, :] = _layernorm(DEEPNORM_ALPHA * h_ref[rs, :] + mix, g_ref[...], b_ref[...])


def _mixout(h, o_na, o_ssd, y5_cm, glu_w, glu_b, w_out, g, b):
    n = h.shape[0]
    tm = MIX_ROW_TILE
    row = lambda wd: pl.BlockSpec((tm, wd), lambda i: (i, 0))
    return pl.pallas_call(
        _mixout_kernel, grid=(n // tm,),
        in_specs=[row(D_MODEL), row(NA_DIM), row(SSD_DIM),
                  pl.BlockSpec((tm // S5_CHUNK, S5_CHUNK * S5_DIM), lambda i: (i, 0)),
                  _const_spec((S5_DIM, S5_DIM)), _const_spec((1, S5_DIM)),
                  _const_spec((D_MODEL, D_MODEL)), _const_spec((1, D_MODEL)), _const_spec((1, D_MODEL))],
        out_specs=row(D_MODEL), out_shape=jax.ShapeDtypeStruct((n, D_MODEL), F32),
        scratch_shapes=[pltpu.VMEM((S5_DIM // LANES, tm, LANES), F32)],
        compiler_params=_params("parallel"), name="mix_out",
    )(h, o_na, o_ssd, y5_cm, glu_w, glu_b, w_out, g, b)


def _kvproj_kernel(m_ref, wk_ref, wv_ref, k_ref, v_ref):
    mb = m_ref[...].astype(BF16)
    k_ref[...] = jnp.dot(mb, wk_ref[...], preferred_element_type=F32).astype(k_ref.dtype)
    v_ref[...] = jnp.dot(mb, wv_ref[...], preferred_element_type=F32).astype(v_ref.dtype)


def _kvproj(mem2, wk, wv):
    n = mem2.shape[0]
    tm = min(ROW_TILE, n)
    row = pl.BlockSpec((tm, D_MODEL), lambda i: (i, 0))
    return pl.pallas_call(
        _kvproj_kernel, grid=(n // tm,),
        in_specs=[row, _const_spec((D_MODEL, D_MODEL)), _const_spec((D_MODEL, D_MODEL))],
        out_specs=[row, row], out_shape=[jax.ShapeDtypeStruct((n, D_MODEL), BF16)] * 2,
        compiler_params=_params("parallel"), name="xa_kv_proj",
    )(mem2, wk, wv)


def _xattn_kernel(h_ref, k_ref, v_ref, wq_ref, wo_ref, g_ref, b_ref, o_ref, oh_ref):
    half = h_ref.shape[0] // 2
    for r in range(2):
        rs = slice(r * half, (r + 1) * half)
        h = h_ref[rs, :]
        q = jnp.dot(h.astype(BF16), wq_ref[...], preferred_element_type=F32) * (XA_HEAD_DIM ** -0.5)
        qb = q.astype(BF16)
        for hd in range(XA_HEADS):
            sl = slice(hd * XA_HEAD_DIM, (hd + 1) * XA_HEAD_DIM)
            s = lax.dot_general(qb[:, sl], k_ref[:, sl], (((1,), (1,)), ((), ())), preferred_element_type=F32)
            m = jnp.max(s, axis=-1, keepdims=True)
            p = jnp.exp(s - m)
            l = jnp.sum(p, axis=-1, keepdims=True)
            o = jnp.dot(p.astype(BF16), v_ref[:, sl], preferred_element_type=F32) / l
            oh_ref[rs, sl] = o.astype(oh_ref.dtype)
        xa = jnp.dot(oh_ref[rs, :], wo_ref[...], preferred_element_type=F32)
        o_ref[rs, :] = _layernorm(DEEPNORM_ALPHA * h + xa, g_ref[...], b_ref[...])


def _xattn(h, k3, v3, wq, wo, g, b, bsz, t):
    m = k3.shape[1]
    h3 = h.reshape(bsz, t, D_MODEL)
    tm = XA_ROW_TILE
    row = pl.BlockSpec((None, tm, D_MODEL), lambda bb, i: (bb, i, 0))
    kv = pl.BlockSpec((None, m, D_MODEL), lambda bb, i: (bb, 0, 0))
    const = lambda shape: pl.BlockSpec(shape, lambda bb, i: (0, 0))
    out = pl.pallas_call(
        _xattn_kernel, grid=(bsz, t // tm),
        in_specs=[row, kv, kv, const((D_MODEL, D_MODEL)), const((D_MODEL, D_MODEL)),
                  const((1, D_MODEL)), const((1, D_MODEL))],
        out_specs=row, out_shape=jax.ShapeDtypeStruct((bsz, t, D_MODEL), F32),
        scratch_shapes=[pltpu.VMEM((tm, D_MODEL), BF16)],
        compiler_params=_params("parallel", "parallel"), name="cross_attn",
    )(h3, k3, v3, wq, wo, g, b)
    return out.reshape(bsz * t, D_MODEL)


FF_TILE = 1024


def _mlp_kernel(h_ref, w1_ref, w2_ref, g_ref, b_ref, o_ref, a_ref):
    half = h_ref.shape[0] // 2
    for r in range(2):
        rs = slice(r * half, (r + 1) * half)
        h = h_ref[rs, :]
        hb = h.astype(BF16)
        for c in range(D_FF // FF_TILE):
            sl = slice(c * FF_TILE, (c + 1) * FF_TILE)
            a = jnp.maximum(jnp.dot(hb, w1_ref[:, sl], preferred_element_type=F32), 0.0)
            a_ref[rs, sl] = (a * a).astype(a_ref.dtype)
        ff = jnp.dot(a_ref[rs, :], w2_ref[...], preferred_element_type=F32)
        o_ref[rs, :] = _layernorm(DEEPNORM_ALPHA * h + ff, g_ref[...], b_ref[...])


def _mlp(h, w1, w2, g, b):
    n = h.shape[0]
    row = pl.BlockSpec((MLP_ROW_TILE, D_MODEL), lambda i: (i, 0))
    return pl.pallas_call(
        _mlp_kernel, grid=(n // MLP_ROW_TILE,),
        in_specs=[row,
                  pl.BlockSpec((D_MODEL, D_FF), lambda i: (0, 0), pipeline_mode=pl.Buffered(1)),
                  pl.BlockSpec((D_FF, D_MODEL), lambda i: (0, 0), pipeline_mode=pl.Buffered(1)),
                  _const_spec((1, D_MODEL)), _const_spec((1, D_MODEL))],
        out_specs=row, out_shape=jax.ShapeDtypeStruct((n, D_MODEL), F32),
        scratch_shapes=[pltpu.VMEM((MLP_ROW_TILE, D_FF), BF16)],
        compiler_params=_params("parallel"), name="mlp",
    )(h, w1, w2, g, b)


def _permute_w_in(w):
    dt_lo = 3 * NA_DIM + SSD_DIM + SSD_CONV_DIM
    dt_hi = dt_lo + 2 * SSD_HEADS
    pad = jnp.zeros((w.shape[0], DT_PAD - 2 * SSD_HEADS), w.dtype)
    return jnp.concatenate([w[:, :dt_lo], w[:, dt_hi:], w[:, dt_lo:dt_hi], pad], axis=1).astype(BF16)


def kernel(x, mem, ln_in_g, ln_in_b, w_in, na_rpb, ssd_conv_w, ssd_conv_b, ssd_dt_bias, ssd_a_log, ssd_d,
           ssd_norm_w, s5_lam_re, s5_lam_im, s5_log_dt, s5_b_re, s5_b_im, s5_c_re, s5_c_im, s5_d, s5_glu_w,
           s5_glu_b, w_mix_out, ln_mix_g, ln_mix_b, xa_wq, xa_wk, xa_wv, xa_wo, ln_xa_g, ln_xa_b, mlp_w1,
           mlp_w2, ln_mlp_g, ln_mlp_b):
    bsz, t, _ = x.shape
    n = bsz * t
    rows = t // GRID_W
    assert all(t % tile == 0 for tile in (IN_ROW_TILE, MIX_ROW_TILE, XA_ROW_TILE, MLP_ROW_TILE))
    assert t % (SSD_BLOCK_CHUNKS * SSD_CHUNK) == 0 and rows % NA_Q_ROWS == 0
    assert rows >= NA_K_ROWS and t % S5_CHUNK == 0
    row1 = lambda v: v.reshape(1, -1).astype(F32)
    mem2 = mem.reshape(-1, D_MODEL)
    h = x.reshape(n, D_MODEL)
    for l in range(DEPTH):
        dt_bias = jnp.pad(row1(ssd_dt_bias[l]), ((0, 0), (0, DT_PAD - 2 * SSD_HEADS)))
        outs = _inproj(h, row1(ln_in_g), row1(ln_in_b), _permute_w_in(w_in[l]), ssd_conv_w[l].astype(F32),
                       row1(ssd_conv_b[l]), dt_bias, t, apply_ln=(l == 0))
        qkv, z, xc, bt, u_cm, dtt = outs[:6]
        if l == 0:
            h = outs[6]
        o_na = _na(qkv, _na_bias_tables(na_rpb[l].astype(F32), rows), bsz, t)

        xc3 = xc.reshape(bsz, t, SSD_CONV_DIM)
        a_log16 = row1(ssd_a_log[l])
        prev_f, prev_b = _ssd_states(xc3, bt, dtt, a_log16)
        o_ssd = _ssd_out(xc3, z.reshape(bsz, t, SSD_DIM), dtt, prev_f, prev_b, a_log16,
                         row1(jnp.repeat(ssd_d[l], HEAD_DIM)), row1(ssd_norm_w[l]))

        tables = _s5_tables(s5_lam_re[l].astype(F32), s5_lam_im[l].astype(F32), s5_log_dt[l].astype(F32),
                            s5_b_re[l].astype(F32), s5_b_im[l].astype(F32), s5_c_re[l].astype(F32),
                            s5_c_im[l].astype(F32))
        y5 = _s5(u_cm, tables, row1(s5_d[l]), bsz, t)

        h = _mixout(h, o_na, o_ssd, y5, s5_glu_w[l].astype(BF16), row1(s5_glu_b[l]),
                    w_mix_out[l].astype(BF16), row1(ln_mix_g[l]), row1(ln_mix_b[l]))
        k2, v2 = _kvproj(mem2, xa_wk[l].astype(BF16), xa_wv[l].astype(BF16))
        h = _xattn(h, k2.reshape(bsz, -1, D_MODEL), v2.reshape(bsz, -1, D_MODEL), xa_wq[l].astype(BF16),
                   xa_wo[l].astype(BF16), row1(ln_xa_g[l]), row1(ln_xa_b[l]), bsz, t)
        h = _mlp(h, mlp_w1[l].astype(BF16), mlp_w2[l].astype(BF16), row1(ln_mlp_g[l]), row1(ln_mlp_b[l]))
    return h.reshape(bsz, t, D_MODEL)
```

```python
import functools
import math

import jax
import jax.numpy as jnp
import numpy as np
from jax import lax
from jax.experimental import pallas as pl
from jax.experimental.pallas import tpu as pltpu

F32 = jnp.float32
BF16 = jnp.bfloat16
HIGHEST = lax.Precision.HIGHEST

D_MODEL = 1024
DEPTH = 2
GRID_W = 64
HEAD_DIM = 64
NA_DIM = 256
NA_HEADS = 4
NA_WIN_ROWS = 8
NA_WIN_COLS = 16
SSD_DIM = 512
SSD_HEADS = 8
SSD_GROUPS = 2
SSD_STATE = 64
SSD_CONV = 5
SSD_CHUNK = 128
SSD_CONV_DIM = SSD_DIM + 2 * SSD_GROUPS * SSD_STATE
S5_DIM = 256
S5_GROUP_CH = 16
S5_GROUPS = 16
S5_STATE = 64
XA_HEADS = 4
XA_HEAD_DIM = 256
D_FF = 4096
LN_EPS = 1e-5
NEG_BIG = -1e30
DEEPNORM_ALPHA = (2 * DEPTH) ** 0.25

ROW_TILE = 512
XA_ROW_TILE = 2048
MIX_ROW_TILE = 1024
MLP_ROW_TILE = 1024
IN_ROW_TILE = 1024
NA_Q_ROWS = 4
NA_K_ROWS = NA_Q_ROWS + NA_WIN_ROWS
NA_BLOCKS_PER_STEP = 4
SSD_BLOCK_CHUNKS = 16
S5_CHUNK = 16
LANES = 128
DT_PAD = LANES
VMEM_LIMIT = 56 * 1024 * 1024


def _params(*sem):
    return pltpu.CompilerParams(dimension_semantics=sem, vmem_limit_bytes=VMEM_LIMIT)


def _layernorm(x, g, b):
    mu = jnp.mean(x, axis=-1, keepdims=True)
    xc = x - mu
    var = jnp.mean(xc * xc, axis=-1, keepdims=True)
    return xc * lax.rsqrt(var + LN_EPS) * g + b


def _sigmoid(x):
    return 0.5 + 0.5 * jnp.tanh(0.5 * x)


def _silu(x):
    return x * _sigmoid(x)


def _softplus(x):
    return jnp.maximum(x, 0.0) + jnp.log1p(jnp.exp(-jnp.abs(x)))


def _const_spec(shape):
    n = len(shape)
    return pl.BlockSpec(shape, lambda *_: (0,) * n)


IN_QKV, IN_Z, IN_XBC, IN_U, IN_DT = (0, 768), (768, 1280), (1280, 2048), (2048, 2304), (2304, 2304 + DT_PAD)
PROJ_HALO = 16


def _inproj_kernel(x_ref, xp_ref, xn_ref, g_ref, b_ref, w_ref, cw_ref, cb_ref, dtb_ref,
                   qkv_ref, z_ref, xc_ref, bt_ref, u_ref, dtt_ref, *rest, apply_ln, tiles_per_seq):
    xe_ref, us_ref = rest[-2:]
    tm, hl = IN_ROW_TILE, PROJ_HALO
    pos = pl.program_id(0) % tiles_per_seq
    x, xp, xn = x_ref[...], xp_ref[...], xn_ref[...]
    if apply_ln:
        x = _layernorm(x, g_ref[...], b_ref[...])
        xp = _layernorm(xp, g_ref[...], b_ref[...])
        xn = _layernorm(xn, g_ref[...], b_ref[...])
        rest[0][...] = x
    xe_ref[0:hl, :] = xp.astype(BF16)
    xe_ref[hl:hl + tm, :] = x.astype(BF16)
    xe_ref[hl + tm:, :] = xn.astype(BF16)

    half = tm // 2
    pad = SSD_CONV // 2
    n_slab = S5_DIM // LANES
    for r in range(2):
        rs = slice(r * half, (r + 1) * half)
        xbc = jnp.dot(xe_ref[r * half:r * half + half + 2 * hl, :], w_ref[:, IN_XBC[0]:IN_XBC[1]],
                      preferred_element_type=F32)
        before = jnp.where(pos == 0, 0.0, xbc[0:hl]) if r == 0 else xbc[0:hl]
        after = jnp.where(pos == tiles_per_seq - 1, 0.0, xbc[hl + half:]) if r == 1 else xbc[hl + half:]
        xbc = jnp.concatenate([before, xbc[hl:hl + half], after], axis=0)

        xb = xe_ref[hl + r * half:hl + (r + 1) * half, :]

        def proj(cols):
            return jnp.dot(xb, w_ref[:, cols[0]:cols[1]], preferred_element_type=F32)

        qkv_ref[rs, :] = proj(IN_QKV).astype(qkv_ref.dtype)
        z_ref[rs, :] = proj(IN_Z)
        u = proj(IN_U)
        dt = _softplus(proj(IN_DT) + dtb_ref[...])
        dtt_ref[:, rs] = dt.T[:2 * SSD_HEADS, :]

        acc = cb_ref[...]
        ext = half + 2 * hl
        for k in range(SSD_CONV):
            shifted = xbc if k == pad else pltpu.roll(xbc, (pad - k) % ext, 0)
            acc = acc + shifted[hl:hl + half] * cw_ref[k:k + 1, :]
        xc = _silu(acc)
        xc_ref[rs, :] = xc.astype(xc_ref.dtype)
        bt_ref[:, rs] = xc[:, SSD_DIM:SSD_DIM + SSD_GROUPS * SSD_STATE].T.astype(bt_ref.dtype)

        cpr = half // S5_CHUNK
        for k in range(n_slab):
            us_ref[r, k] = u[:, k * LANES:(k + 1) * LANES]
        for s in range(S5_CHUNK):
            for k in range(n_slab):
                u_ref[r * cpr:(r + 1) * cpr, s * S5_DIM + k * LANES:s * S5_DIM + (k + 1) * LANES] = \
                    us_ref[r, k, pl.ds(s, cpr, stride=S5_CHUNK), :]


def _inproj(x, g, b, w, conv_w, conv_b, dt_bias, t, apply_ln):
    n = x.shape[0]
    tm, hl = IN_ROW_TILE, PROJ_HALO
    per = tm // hl
    row = lambda wd: pl.BlockSpec((tm, wd), lambda i: (i, 0))
    bn = SSD_GROUPS * SSD_STATE
    out_shape = [jax.ShapeDtypeStruct((n, 3 * NA_DIM), BF16), jax.ShapeDtypeStruct((n, SSD_DIM), F32),
                 jax.ShapeDtypeStruct((n, SSD_CONV_DIM), BF16), jax.ShapeDtypeStruct((bn, n), BF16),
                 jax.ShapeDtypeStruct((n // S5_CHUNK, S5_CHUNK * S5_DIM), F32),
                 jax.ShapeDtypeStruct((2 * SSD_HEADS, n), F32)]
    out_specs = [row(3 * NA_DIM), row(SSD_DIM), row(SSD_CONV_DIM), pl.BlockSpec((bn, tm), lambda i: (0, i)),
                 pl.BlockSpec((tm // S5_CHUNK, S5_CHUNK * S5_DIM), lambda i: (i, 0)),
                 pl.BlockSpec((2 * SSD_HEADS, tm), lambda i: (0, i))]
    if apply_ln:
        out_shape.append(jax.ShapeDtypeStruct((n, D_MODEL), F32))
        out_specs.append(row(D_MODEL))
    return pl.pallas_call(
        functools.partial(_inproj_kernel, apply_ln=apply_ln, tiles_per_seq=t // tm),
        grid=(n // tm,),
        in_specs=[row(D_MODEL),
                  pl.BlockSpec((hl, D_MODEL), lambda i: (jnp.maximum(i * per - 1, 0), 0)),
                  pl.BlockSpec((hl, D_MODEL), lambda i: (jnp.minimum((i + 1) * per, n // hl - 1), 0)),
                  _const_spec((1, D_MODEL)), _const_spec((1, D_MODEL)), _const_spec(w.shape),
                  _const_spec((SSD_CONV, SSD_CONV_DIM)), _const_spec((1, SSD_CONV_DIM)), _const_spec((1, DT_PAD))],
        out_specs=out_specs, out_shape=out_shape,
        scratch_shapes=[pltpu.VMEM((tm + 2 * hl, D_MODEL), BF16),
                        pltpu.VMEM((2, S5_DIM // LANES, tm // 2, LANES), F32)],
        compiler_params=_params("parallel"), name="in_proj",
    )(x, x, x, g, b, w, conv_w, conv_b, dt_bias)


def _na_bias_tables(rpb, rows):
    n_ri, n_ci = 2 * NA_WIN_ROWS - 1, 2 * NA_WIN_COLS - 1
    qr = np.arange(NA_Q_ROWS)
    kr = np.arange(NA_K_ROWS)
    c = np.arange(GRID_W)
    c0 = np.clip(c - NA_WIN_COLS // 2, 0, GRID_W - NA_WIN_COLS)
    col_ok = (c[None, :] >= c0[:, None]) & (c[None, :] < c0[:, None] + NA_WIN_COLS)
    ci = np.clip(c[None, :] - c[:, None], -(NA_WIN_COLS - 1), NA_WIN_COLS - 1) + (NA_WIN_COLS - 1)
    onehot_ci = (ci[None] == np.arange(n_ci)[:, None, None]).astype(np.float32)
    onehot_ri, ok = [], []
    for blk_row in (0, NA_Q_ROWS, rows - NA_Q_ROWS):
        start = min(max(blk_row - NA_WIN_ROWS // 2, 0), rows - NA_K_ROWS)
        r = blk_row + qr
        r0 = np.clip(r - NA_WIN_ROWS // 2, 0, rows - NA_WIN_ROWS)
        key_row = start + kr
        row_ok = (key_row[None, :] >= r0[:, None]) & (key_row[None, :] < r0[:, None] + NA_WIN_ROWS)
        ri = key_row[None, :] - r[:, None] + (NA_WIN_ROWS - 1)
        onehot_ri.append(((ri[..., None] == np.arange(n_ri)) & row_ok[..., None]).astype(np.float32))
        ok.append(row_ok[:, None, :, None] & col_ok[None, :, None, :])
    col_tab = jnp.einsum('hrc,cqk->hrqk', rpb, jnp.asarray(onehot_ci), precision=HIGHEST)
    bias = jnp.einsum('aqjr,hrwk->ahqwjk', jnp.asarray(np.stack(onehot_ri)), col_tab, precision=HIGHEST)
    bias = jnp.where(jnp.asarray(np.stack(ok))[:, None], bias, NEG_BIG)
    return bias.reshape(3, NA_HEADS, NA_Q_ROWS * GRID_W, NA_K_ROWS * GRID_W).astype(F32)


def _na_kernel(q_ref, k_ref, v_ref, bias_ref, o_ref, *, rows):
    nblk = rows // NA_Q_ROWS
    nk = NA_K_ROWS * GRID_W
    tq = NA_Q_ROWS * GRID_W
    for r in range(NA_BLOCKS_PER_STEP):
        i = pl.program_id(1) * NA_BLOCKS_PER_STEP + r
        case = jnp.where(i == 0, 0, jnp.where(i == nblk - 1, 2, 1))
        start_row = jnp.clip(i * NA_Q_ROWS - NA_WIN_ROWS // 2, 0, rows - NA_K_ROWS)
        start = pl.multiple_of(start_row * GRID_W, GRID_W)
        rs = slice(r * tq, (r + 1) * tq)
        q = q_ref[rs, :] * (HEAD_DIM ** -0.5)
        kw = k_ref[pl.ds(start, nk), :]
        vw = v_ref[pl.ds(start, nk), :]
        for h in range(NA_HEADS):
            sl = slice(h * HEAD_DIM, (h + 1) * HEAD_DIM)
            s = lax.dot_general(q[:, sl], kw[:, sl], (((1,), (1,)), ((), ())), preferred_element_type=F32)
            s = s + bias_ref[case, h]
            m = jnp.max(s, axis=-1, keepdims=True)
            p = jnp.exp(s - m)
            l = jnp.sum(p, axis=-1, keepdims=True)
            o = jnp.dot(p.astype(BF16), vw[:, sl], preferred_element_type=F32)
            o_ref[rs, sl] = (o / l).astype(o_ref.dtype)


def _na(qkv, bias, bsz, t):
    rows = t // GRID_W
    tq = NA_BLOCKS_PER_STEP * NA_Q_ROWS * GRID_W
    qkv3 = qkv.reshape(bsz, t, 3 * NA_DIM)
    out = pl.pallas_call(
        functools.partial(_na_kernel, rows=rows),
        grid=(bsz, t // tq),
        in_specs=[pl.BlockSpec((None, tq, NA_DIM), lambda b, i: (b, i, 0)),
                  pl.BlockSpec((None, t, NA_DIM), lambda b, i: (b, 0, 1)),
                  pl.BlockSpec((None, t, NA_DIM), lambda b, i: (b, 0, 2)),
                  pl.BlockSpec(bias.shape, lambda b, i: (0, 0, 0, 0), pipeline_mode=pl.Buffered(1))],
        out_specs=pl.BlockSpec((None, tq, NA_DIM), lambda b, i: (b, i, 0)),
        out_shape=jax.ShapeDtypeStruct((bsz, t, NA_DIM), BF16),
        compiler_params=_params("parallel", "arbitrary"), name="na_attn",
    )(qkv3, qkv3, qkv3, bias)
    return out.reshape(bsz * t, NA_DIM)


def _tri(n, lower):
    r = lax.broadcasted_iota(jnp.int32, (n, n), 0)
    c = lax.broadcasted_iota(jnp.int32, (n, n), 1)
    return (r >= c) if lower else (r <= c)


def _split3(x):
    hi = x.astype(BF16).astype(F32)
    r = x - hi
    mid = r.astype(BF16).astype(F32)
    lo = (r - mid).astype(BF16).astype(F32)
    return hi, mid, lo


def _dot01_left(m01, x):
    return sum(jnp.dot(m01, p.astype(BF16), preferred_element_type=F32) for p in _split3(x))


def _dot01_right(x, m01):
    return sum(jnp.dot(p.astype(BF16), m01, preferred_element_type=F32) for p in _split3(x))


def _head_expand():
    r = lax.broadcasted_iota(jnp.int32, (SSD_HEADS, SSD_DIM), 0)
    c = lax.broadcasted_iota(jnp.int32, (SSD_HEADS, SSD_DIM), 1)
    return (c // HEAD_DIM == r).astype(BF16)


def _ssd_state_kernel(xf_ref, xb_ref, btf_ref, btb_ref, dttf_ref, dttb_ref, alog_ref,
                      pf_ref, pb_ref, sf_ref, sb_ref):
    q = SSD_CHUNK
    nh = SSD_HEADS
    pair_w = 2 * HEAD_DIM

    @pl.when(pl.program_id(1) == 0)
    def _():
        sf_ref[...] = jnp.zeros_like(sf_ref)
        sb_ref[...] = jnp.zeros_like(sb_ref)

    row_id = lax.broadcasted_iota(jnp.int32, (q, q), 0)
    col_id = lax.broadcasted_iota(jnp.int32, (q, q), 1)
    first_half = lax.broadcasted_iota(jnp.int32, (SSD_STATE, pair_w), 1) < HEAD_DIM
    a_col = -jnp.exp(alog_ref[...])
    for d, (x_ref, bt_ref, dtt_ref, prev_ref, state_ref) in enumerate(
            ((xf_ref, btf_ref, dttf_ref, pf_ref, sf_ref), (xb_ref, btb_ref, dttb_ref, pb_ref, sb_ref))):
        backward = d == 1
        hs = slice(d * nh, (d + 1) * nh)
        tri = ((row_id >= col_id) if backward else (row_id <= col_id)).astype(BF16)
        dtr = dtt_ref[hs, :]
        da = dtr * a_col[hs]
        local, decay = [], []
        da_rows = jnp.concatenate([da[:, cc * q:(cc + 1) * q] for cc in range(SSD_BLOCK_CHUNKS)], axis=0)
        cs_rows = _dot01_right(da_rows, tri)
        for cc in range(SSD_BLOCK_CHUNKS):
            rs = slice(cc * q, (cc + 1) * q)
            cs = cs_rows[cc * nh:(cc + 1) * nh]
            tot = cs[:, 0:1] if backward else cs[:, q - 1:q]
            w = jnp.exp(tot - cs) * dtr[:, rs]
            chunk_decay = jnp.exp(tot)
            local.append([])
            decay.append([])
            for pp in range(nh // 2):
                g = (2 * pp) // (nh // SSD_GROUPS)
                bt = bt_ref[g * SSD_STATE:(g + 1) * SSD_STATE, rs].astype(F32)
                xs = x_ref[rs, pp * pair_w:(pp + 1) * pair_w]
                lhs = jnp.concatenate([(bt * w[h:h + 1, :]).astype(BF16) for h in (2 * pp, 2 * pp + 1)], axis=0)
                both = jnp.dot(lhs, xs, preferred_element_type=F32)
                local[cc].append(jnp.where(first_half, both[:SSD_STATE], both[SSD_STATE:]))
                decay[cc].append(jnp.where(first_half, chunk_decay[2 * pp:2 * pp + 1, :],
                                           chunk_decay[2 * pp + 1:2 * pp + 2, :]))
        order = range(SSD_BLOCK_CHUNKS - 1, -1, -1) if backward else range(SSD_BLOCK_CHUNKS)
        for pp in range(nh // 2):
            lanes = slice(pp * pair_w, (pp + 1) * pair_w)
            state = state_ref[:, lanes]
            for cc in order:
                prev_ref[cc, :, lanes] = state.astype(prev_ref.dtype)
                state = state * decay[cc][pp] + local[cc][pp]
            state_ref[:, lanes] = state


def _ssd_states(xc3, bt, dtt, a_log16):
    bsz, t, _ = xc3.shape
    tb = SSD_BLOCK_CHUNKS * SSD_CHUNK
    nblk = t // tb
    bn = SSD_GROUPS * SSD_STATE
    fwd_rows = pl.BlockSpec((None, tb, SSD_CONV_DIM), lambda b, j: (b, j, 0))
    bwd_rows = pl.BlockSpec((None, tb, SSD_CONV_DIM), lambda b, j: (b, nblk - 1 - j, 0))
    fwd_cols = lambda rows: pl.BlockSpec((rows, tb), lambda b, j: (0, b * nblk + j))
    bwd_cols = lambda rows: pl.BlockSpec((rows, tb), lambda b, j: (0, b * nblk + nblk - 1 - j))
    state = jax.ShapeDtypeStruct((bsz, t // SSD_CHUNK, SSD_STATE, SSD_DIM), BF16)
    return pl.pallas_call(
        _ssd_state_kernel, grid=(bsz, nblk),
        in_specs=[fwd_rows, bwd_rows, fwd_cols(bn), bwd_cols(bn), fwd_cols(2 * SSD_HEADS), bwd_cols(2 * SSD_HEADS),
                  _const_spec((2 * SSD_HEADS, 1))],
        out_specs=[pl.BlockSpec((None, SSD_BLOCK_CHUNKS, SSD_STATE, SSD_DIM), lambda b, j: (b, j, 0, 0)),
                   pl.BlockSpec((None, SSD_BLOCK_CHUNKS, SSD_STATE, SSD_DIM),
                                lambda b, j: (b, nblk - 1 - j, 0, 0))],
        out_shape=[state, state],
        scratch_shapes=[pltpu.VMEM((SSD_STATE, SSD_DIM), F32), pltpu.VMEM((SSD_STATE, SSD_DIM), F32)],
        compiler_params=_params("parallel", "arbitrary"), name="ssd_state",
    )(xc3, xc3, bt, bt, dtt, dtt, a_log16.reshape(2 * SSD_HEADS, 1))


def _ssd_out_kernel(x_ref, z_ref, dtt_ref, pf_ref, pb_ref, alog_col_ref, dskip_ref, nw_ref, o_ref):
    q = SSD_CHUNK
    nh = SSD_HEADS
    row_id = lax.broadcasted_iota(jnp.int32, (q, q), 0)
    col_id = lax.broadcasted_iota(jnp.int32, (q, q), 1)
    lower = row_id >= col_id
    eye = row_id == col_id
    first_half = lax.broadcasted_iota(jnp.int32, (q, 2 * HEAD_DIM), 1) < HEAD_DIM
    tri_l = lower.astype(BF16)
    tri_u = (row_id <= col_id).astype(BF16)
    a_col = -jnp.exp(alog_col_ref[...])
    dtt_all = dtt_ref[...]
    da = dtt_all * a_col
    da_rows = jnp.concatenate([da[:, cc * q:(cc + 1) * q] for cc in range(SSD_BLOCK_CHUNKS)], axis=0)
    cs_rows_f, cs_rows_b = _dot01_right(da_rows, tri_u), _dot01_right(da_rows, tri_l)
    cs_cols_f, cs_cols_b = cs_rows_f.T, cs_rows_b.T
    e_cols_f, e_cols_b = jnp.exp(cs_cols_f), jnp.exp(cs_cols_b)
    log_dt = jnp.log(dtt_all)
    ldt_rows = jnp.concatenate([log_dt[:, cc * q:(cc + 1) * q] for cc in range(SSD_BLOCK_CHUNKS)], axis=0)
    rp_rows_f, rp_rows_b = cs_rows_f - ldt_rows, cs_rows_b - ldt_rows
    for cc in range(SSD_BLOCK_CHUNKS):
        rs = slice(cc * q, (cc + 1) * q)
        dtr = dtt_all[:, rs]
        hrow = slice(cc * 2 * nh, (cc + 1) * 2 * nh)
        cs_row_f, cs_row_b = rp_rows_f[hrow], rp_rows_b[hrow]
        cs_col_f, cs_col_b = cs_cols_f[:, hrow], cs_cols_b[:, hrow]
        e_col_f, e_col_b = e_cols_f[:, hrow], e_cols_b[:, hrow]
        bc = x_ref[rs, SSD_DIM:SSD_CONV_DIM]
        groups = []
        for g in range(SSD_GROUPS):
            bg = bc[:, g * SSD_STATE:(g + 1) * SSD_STATE]
            cg = bc[:, (SSD_GROUPS + g) * SSD_STATE:(SSD_GROUPS + g + 1) * SSD_STATE]
            cb = lax.dot_general(cg, bg, (((1,), (1,)), ((), ())), preferred_element_type=F32)
            c2 = jnp.concatenate([cg, cg], axis=1).astype(F32)
            pairs = []
            for pp in range(SSD_HEADS // SSD_GROUPS // 2):
                h0 = g * (SSD_HEADS // SSD_GROUPS) + 2 * pp
                lanes = slice(h0 * HEAD_DIM, (h0 + 2) * HEAD_DIM)
                rhs = jnp.concatenate([x_ref[rs, lanes], pf_ref[cc, :, lanes], pb_ref[cc, :, lanes]], axis=0)
                lhs = []
                for h in (h0, h0 + 1):
                    seg_f = cs_col_f[:, h:h + 1] - cs_row_f[h:h + 1, :]
                    seg_b = cs_col_b[:, nh + h:nh + h + 1] - cs_row_b[nh + h:nh + h + 1, :]
                    lmat = (jnp.exp(jnp.where(lower, seg_f, seg_b))
                            + jnp.where(eye, dtr[nh + h:nh + h + 1, :], 0.0))
                    e2 = jnp.where(first_half, e_col_f[:, h:h + 1], e_col_b[:, nh + h:nh + h + 1])
                    lhs.append(jnp.concatenate([(cb * lmat).astype(BF16), (c2 * e2).astype(BF16)], axis=1))
                both = jnp.dot(jnp.concatenate(lhs, axis=0), rhs, preferred_element_type=F32)
                pairs.append(jnp.where(first_half, both[:q], both[q:]))
            groups.append(jnp.concatenate(pairs, axis=1))
        y = jnp.concatenate(groups, axis=1) + dskip_ref[...] * x_ref[rs, :SSD_DIM].astype(F32)
        yg = y * _silu(z_ref[rs, :])
        ms = jnp.mean(yg * yg, axis=-1, keepdims=True)
        o_ref[rs, :] = (yg * lax.rsqrt(ms + LN_EPS) * nw_ref[...]).astype(o_ref.dtype)


def _ssd_out(xc3, z3, dtt, prev_f, prev_b, a_log16, d_skip, norm_w):
    bsz, t, _ = xc3.shape
    tb = SSD_BLOCK_CHUNKS * SSD_CHUNK
    nblk = t // tb
    blk = lambda wd: pl.BlockSpec((None, tb, wd), lambda b, j: (b, j, 0))
    prev = pl.BlockSpec((None, SSD_BLOCK_CHUNKS, SSD_STATE, SSD_DIM), lambda b, j: (b, j, 0, 0))
    out = pl.pallas_call(
        _ssd_out_kernel, grid=(bsz, nblk),
        in_specs=[blk(SSD_CONV_DIM), blk(SSD_DIM),
                  pl.BlockSpec((2 * SSD_HEADS, tb), lambda b, j: (0, b * nblk + j)),
                  prev, prev, _const_spec((2 * SSD_HEADS, 1)),
                  _const_spec((1, SSD_DIM)), _const_spec((1, SSD_DIM))],
        out_specs=blk(SSD_DIM),
        out_shape=jax.ShapeDtypeStruct((bsz, t, SSD_DIM), BF16),
        compiler_params=_params("parallel", "parallel"), name="ssd_out",
    )(xc3, z3, dtt, prev_f, prev_b, a_log16.reshape(2 * SSD_HEADS, 1), d_skip, norm_w)
    return out.reshape(bsz * t, SSD_DIM)


def _s5_tables(lam_re, lam_im, log_dt, b_re, b_im, c_re, c_im):
    lc = S5_CHUNK
    hp = HIGHEST
    pw_re, pw_im, bb_re, bb_im = [], [], [], []
    for d in range(2):
        dt = jnp.exp(log_dt[d])[:, None]
        lr, li = lam_re[d], lam_im[d]
        mag = jnp.exp(lr * dt)
        ar, ai = mag * jnp.cos(li * dt), mag * jnp.sin(li * dt)
        den = lr * lr + li * li
        fr = ((ar - 1.0) * lr + ai * li) / den
        fi = (ai * lr - (ar - 1.0) * li) / den
        bb_re.append(fr[..., None] * b_re[d] - fi[..., None] * b_im[d])
        bb_im.append(fr[..., None] * b_im[d] + fi[..., None] * b_re[d])
        k = jnp.arange(lc + 1, dtype=F32)[:, None, None]
        pw_re.append(jnp.exp(k * (lr * dt)) * jnp.cos(k * (li * dt)))
        pw_im.append(jnp.exp(k * (lr * dt)) * jnp.sin(k * (li * dt)))

    def kern(d):
        wr = pw_re[d][:lc, :, :, None] * bb_re[d][None] - pw_im[d][:lc, :, :, None] * bb_im[d][None]
        wi = pw_re[d][:lc, :, :, None] * bb_im[d][None] + pw_im[d][:lc, :, :, None] * bb_re[d][None]
        return (jnp.einsum('ghp,kgpj->gjkh', c_re[d], wr, precision=hp)
                - jnp.einsum('ghp,kgpj->gjkh', c_im[d], wi, precision=hp))

    kf, kb = kern(0), kern(1)
    gc = S5_GROUP_CH
    k_all = jnp.concatenate([jnp.flip(kb[:, :, 1:], axis=2), kf[:, :, :1] + kb[:, :, :1], kf[:, :, 1:]], axis=2)
    k_all = k_all.reshape(S5_GROUPS, gc, (2 * lc - 1) * gc)
    toep = jnp.stack([k_all[:, :, (lc - 1 - s) * gc:(2 * lc - 1 - s) * gc] for s in range(lc)], axis=1)
    toep = toep.reshape(S5_GROUPS, lc * gc, lc * gc)
    rising = lambda lo: (lambda pw: pw[lo:lo + lc])
    falling = lambda lo: (lambda pw: jnp.flip(pw[lo:lo + lc], axis=0))

    def state_in(d, take):
        pr, pi = take(pw_re[d]), take(pw_im[d])
        wr = pr[:, :, :, None] * bb_re[d][None] - pi[:, :, :, None] * bb_im[d][None]
        wi = pr[:, :, :, None] * bb_im[d][None] + pi[:, :, :, None] * bb_re[d][None]
        to_rows = lambda w: w.transpose(1, 0, 3, 2).reshape(S5_GROUPS, lc * S5_GROUP_CH, S5_STATE)
        return to_rows(wr), to_rows(wi)

    f_re, f_im = state_in(0, falling(0))
    b_re_, b_im_ = state_in(1, rising(0))
    m_state = jnp.concatenate([f_re, b_re_, f_im, b_im_], axis=-1)

    def state_out(d, take):
        pr, pi = take(pw_re[d]), take(pw_im[d])
        cpr = c_re[d][None] * pr[:, :, None, :] - c_im[d][None] * pi[:, :, None, :]
        cpi = c_re[d][None] * pi[:, :, None, :] + c_im[d][None] * pr[:, :, None, :]
        to_cols = lambda w: w.transpose(1, 3, 0, 2).reshape(S5_GROUPS, S5_STATE, lc * S5_GROUP_CH)
        return to_cols(cpr), -to_cols(cpi)

    of_re, of_im = state_out(0, rising(1))
    ob_re, ob_im = state_out(1, falling(1))
    m_off = jnp.concatenate([of_re, ob_re, of_im, ob_im], axis=1)
    dec_re = jnp.concatenate([pw_re[0][lc], pw_re[1][lc]], axis=-1)[:, None, :]
    dec_im = jnp.concatenate([pw_im[0][lc], pw_im[1][lc]], axis=-1)[:, None, :]
    return toep.astype(BF16), m_state.astype(BF16), m_off.astype(BF16), dec_re, dec_im


S5_W = S5_CHUNK * S5_GROUP_CH
S5_RELAYOUT_CHUNKS = 64
S5_GROUPS_PER_STEP = 2


def _s5_group_kernel(u_ref, o_ref):
    x = u_ref[...]
    for g in range(S5_GROUPS):
        lo = g * S5_GROUP_CH
        pieces = [x[:, s * S5_DIM + lo:s * S5_DIM + lo + S5_GROUP_CH] for s in range(S5_CHUNK)]
        o_ref[g] = jnp.concatenate(pieces, axis=-1).astype(o_ref.dtype)


def _s5_ungroup_kernel(y_ref, u_ref, d_ref, o_ref):
    ys = [y_ref[g] for g in range(S5_GROUPS)]
    for l in range(S5_CHUNK):
        cols = slice(l * S5_DIM, (l + 1) * S5_DIM)
        pieces = [y[:, l * S5_GROUP_CH:(l + 1) * S5_GROUP_CH] for y in ys]
        o_ref[:, cols] = jnp.concatenate(pieces, axis=-1) + d_ref[...] * u_ref[:, cols]


def _s5_kernel(u_ref, toep_ref, mst_ref, moff_ref, are_ref, aim_ref, y_ref, s_ref, e_ref, *, nc, bsz):
    ng = S5_GROUPS_PER_STEP
    half = S5_STATE
    w = S5_W
    for gi in range(ng):
        for b in range(bsz):
            ub = u_ref[gi, :, b * w:(b + 1) * w]
            y_ref[gi, :, b * w:(b + 1) * w] = jnp.dot(ub, toep_ref[gi], preferred_element_type=F32)
            s = jnp.dot(ub, mst_ref[gi], preferred_element_type=F32)
            for k in range(2):
                s_ref[gi, k, pl.ds(b, nc, stride=bsz), :] = s[:, k * LANES:(k + 1) * LANES]
    ar = [jnp.broadcast_to(are_ref[gi], (bsz, 2 * half)) for gi in range(ng)]
    ai = [jnp.broadcast_to(aim_ref[gi], (bsz, 2 * half)) for gi in range(ng)]
    is_fwd = lax.broadcasted_iota(jnp.int32, (bsz, 2 * half), 1) < half

    def body(i, carry):
        rf = pl.multiple_of(i * bsz, bsz)
        rb = pl.multiple_of((nc - 1 - i) * bsz, bsz)
        out = []
        for gi in range(ng):
            er, ei = carry[gi]
            e_ref[gi, 0, pl.ds(rf, bsz), 0:half] = er[:, :half]
            e_ref[gi, 0, pl.ds(rb, bsz), half:] = er[:, half:]
            e_ref[gi, 1, pl.ds(rf, bsz), 0:half] = ei[:, :half]
            e_ref[gi, 1, pl.ds(rb, bsz), half:] = ei[:, half:]
            sr = jnp.where(is_fwd, s_ref[gi, 0, pl.ds(rf, bsz), :], s_ref[gi, 0, pl.ds(rb, bsz), :])
            si = jnp.where(is_fwd, s_ref[gi, 1, pl.ds(rf, bsz), :], s_ref[gi, 1, pl.ds(rb, bsz), :])
            out.append((ar[gi] * er - ai[gi] * ei + sr, ar[gi] * ei + ai[gi] * er + si))
        return tuple(out)

    zero = jnp.zeros((bsz, 2 * half), F32)
    lax.fori_loop(0, nc, body, tuple((zero, zero) for _ in range(ng)), unroll=2)
    for gi in range(ng):
        for b in range(bsz):
            e = jnp.concatenate([e_ref[gi, k, pl.ds(b, nc, stride=bsz), :] for k in range(2)], axis=-1)
            y_ref[gi, :, b * w:(b + 1) * w] += jnp.dot(e.astype(BF16), moff_ref[gi], preferred_element_type=F32)


def _s5(u_cm, tables, d_skip, bsz, t):
    toep, m_state, m_off, dec_re, dec_im = tables
    nc = t // S5_CHUNK
    w = S5_W
    rc = min(S5_RELAYOUT_CHUNKS, nc)
    u3 = u_cm.reshape(bsz, nc, S5_CHUNK * S5_DIM)
    tok = pl.BlockSpec((None, rc, S5_CHUNK * S5_DIM), lambda b, j: (b, j, 0))
    grouped = pl.BlockSpec((S5_GROUPS, rc, w), lambda b, j: (0, j, b))
    ug = pl.pallas_call(
        _s5_group_kernel, grid=(bsz, nc // rc), in_specs=[tok], out_specs=grouped,
        out_shape=jax.ShapeDtypeStruct((S5_GROUPS, nc, bsz * w), BF16),
        compiler_params=_params("parallel", "parallel"), name="s5_group",
    )(u3)
    ng = S5_GROUPS_PER_STEP
    grp = lambda shape: pl.BlockSpec((ng,) + shape, lambda g: (g, 0, 0))
    yg = pl.pallas_call(
        functools.partial(_s5_kernel, nc=nc, bsz=bsz),
        grid=(S5_GROUPS // ng,),
        in_specs=[grp((nc, bsz * w)), grp((w, w)), grp((w, w)), grp((w, w)),
                  grp((1, 2 * S5_STATE)), grp((1, 2 * S5_STATE))],
        out_specs=grp((nc, bsz * w)),
        out_shape=jax.ShapeDtypeStruct((S5_GROUPS, nc, bsz * w), F32),
        scratch_shapes=[pltpu.VMEM((ng, 2, nc * bsz, LANES), F32), pltpu.VMEM((ng, 2, nc * bsz, LANES), F32)],
        compiler_params=_params("parallel"), name="s5_scan",
    )(ug, toep, m_state, m_off, dec_re, dec_im)
    y = pl.pallas_call(
        _s5_ungroup_kernel, grid=(bsz, nc // rc),
        in_specs=[grouped, tok, pl.BlockSpec((1, S5_DIM), lambda b, j: (0, 0))], out_specs=tok,
        out_shape=jax.ShapeDtypeStruct((bsz, nc, S5_CHUNK * S5_DIM), F32),
        compiler_params=_params("parallel", "parallel"), name="s5_ungroup",
    )(yg, u3, d_skip)
    return y.reshape(bsz * nc, S5_CHUNK * S5_DIM)


def _mixout_kernel(h_ref, na_ref, ssd_ref, y5_ref, gw_ref, gb_ref, wo_ref, g_ref, b_ref, o_ref, ys_ref):
    tm = h_ref.shape[0]
    n_slab = S5_DIM // LANES
    for l in range(S5_CHUNK):
        for k in range(n_slab):
            ys_ref[k, pl.ds(l, tm // S5_CHUNK, stride=S5_CHUNK), :] = \
                y5_ref[:, l * S5_DIM + k * LANES:l * S5_DIM + (k + 1) * LANES]
    half = tm // 2
    for r in range(2):
        rs = slice(r * half, (r + 1) * half)
        y5 = jnp.concatenate([ys_ref[k, rs, :] for k in range(n_slab)], axis=-1)
        gl = 0.5 * y5 * (1.0 + jnp.tanh(math.sqrt(2.0 / math.pi) * (y5 + 0.044715 * (y5 * y5 * y5))))
        gate = jnp.dot(gl.astype(BF16), gw_ref[...], preferred_element_type=F32) + gb_ref[...]
        o5 = gl * _sigmoid(gate)
        mixed = jnp.concatenate([na_ref[rs, :], ssd_ref[rs, :], o5.astype(BF16)], axis=-1)
        mix = jnp.dot(mixed, wo_ref[...], preferred_element_type=F32)
        o_ref[rs, :] = _layernorm(DEEPNORM_ALPHA * h_ref[rs, :] + mix, g_ref[...], b_ref[...])


def _mixout(h, o_na, o_ssd, y5_cm, glu_w, glu_b, w_out, g, b):
    n = h.shape[0]
    tm = MIX_ROW_TILE
    row = lambda wd: pl.BlockSpec((tm, wd), lambda i: (i, 0))
    return pl.pallas_call(
        _mixout_kernel, grid=(n // tm,),
        in_specs=[row(D_MODEL), row(NA_DIM), row(SSD_DIM),
                  pl.BlockSpec((tm // S5_CHUNK, S5_CHUNK * S5_DIM), lambda i: (i, 0)),
                  _const_spec((S5_DIM, S5_DIM)), _const_spec((1, S5_DIM)),
                  _const_spec((D_MODEL, D_MODEL)), _const_spec((1, D_MODEL)), _const_spec((1, D_MODEL))],
        out_specs=row(D_MODEL), out_shape=jax.ShapeDtypeStruct((n, D_MODEL), F32),
        scratch_shapes=[pltpu.VMEM((S5_DIM // LANES, tm, LANES), F32)],
        compiler_params=_params("parallel"), name="mix_out",
    )(h, o_na, o_ssd, y5_cm, glu_w, glu_b, w_out, g, b)


def _kvproj_kernel(m_ref, wk_ref, wv_ref, k_ref, v_ref):
    mb = m_ref[...].astype(BF16)
    k_ref[...] = jnp.dot(mb, wk_ref[...], preferred_element_type=F32).astype(k_ref.dtype)
    v_ref[...] = jnp.dot(mb, wv_ref[...], preferred_element_type=F32).astype(v_ref.dtype)


def _kvproj(mem2, wk, wv):
    n = mem2.shape[0]
    tm = min(ROW_TILE, n)
    row = pl.BlockSpec((tm, D_MODEL), lambda i: (i, 0))
    return pl.pallas_call(
        _kvproj_kernel, grid=(n // tm,),
        in_specs=[row, _const_spec((D_MODEL, D_MODEL)), _const_spec((D_MODEL, D_MODEL))],
        out_specs=[row, row], out_shape=[jax.ShapeDtypeStruct((n, D_MODEL), BF16)] * 2,
        compiler_params=_params("parallel"), name="xa_kv_proj",
    )(mem2, wk, wv)


def _xattn_kernel(h_ref, k_ref, v_ref, wq_ref, wo_ref, g_ref, b_ref, o_ref, oh_ref):
    half = h_ref.shape[0] // 2
    for r in range(2):
        rs = slice(r * half, (r + 1) * half)
        h = h_ref[rs, :]
        q = jnp.dot(h.astype(BF16), wq_ref[...], preferred_element_type=F32) * (XA_HEAD_DIM ** -0.5)
        qb = q.astype(BF16)
        for hd in range(XA_HEADS):
            sl = slice(hd * XA_HEAD_DIM, (hd + 1) * XA_HEAD_DIM)
            s = lax.dot_general(qb[:, sl], k_ref[:, sl], (((1,), (1,)), ((), ())), preferred_element_type=F32)
            m = jnp.max(s, axis=-1, keepdims=True)
            p = jnp.exp(s - m)
            l = jnp.sum(p, axis=-1, keepdims=True)
            o = jnp.dot(p.astype(BF16), v_ref[:, sl], preferred_element_type=F32) / l
            oh_ref[rs, sl] = o.astype(oh_ref.dtype)
        xa = jnp.dot(oh_ref[rs, :], wo_ref[...], preferred_element_type=F32)
        o_ref[rs, :] = _layernorm(DEEPNORM_ALPHA * h + xa, g_ref[...], b_ref[...])


def _xattn(h, k3, v3, wq, wo, g, b, bsz, t):
    m = k3.shape[1]
    h3 = h.reshape(bsz, t, D_MODEL)
    tm = XA_ROW_TILE
    row = pl.BlockSpec((None, tm, D_MODEL), lambda bb, i: (bb, i, 0))
    kv = pl.BlockSpec((None, m, D_MODEL), lambda bb, i: (bb, 0, 0))
    const = lambda shape: pl.BlockSpec(shape, lambda bb, i: (0, 0))
    out = pl.pallas_call(
        _xattn_kernel, grid=(bsz, t // tm),
        in_specs=[row, kv, kv, const((D_MODEL, D_MODEL)), const((D_MODEL, D_MODEL)),
                  const((1, D_MODEL)), const((1, D_MODEL))],
        out_specs=row, out_shape=jax.ShapeDtypeStruct((bsz, t, D_MODEL), F32),
        scratch_shapes=[pltpu.VMEM((tm, D_MODEL), BF16)],
        compiler_params=_params("parallel", "parallel"), name="cross_attn",
    )(h3, k3, v3, wq, wo, g, b)
    return out.reshape(bsz * t, D_MODEL)


FF_TILE = 1024


def _mlp_kernel(h_ref, w1_ref, w2_ref, g_ref, b_ref, o_ref, a_ref):
    half = h_ref.shape[0] // 2
    for r in range(2):
        rs = slice(r * half, (r + 1) * half)
        h = h_ref[rs, :]
        hb = h.astype(BF16)
        for c in range(D_FF // FF_TILE):
            sl = slice(c * FF_TILE, (c + 1) * FF_TILE)
            a = jnp.maximum(jnp.dot(hb, w1_ref[:, sl], preferred_element_type=F32), 0.0)
            a_ref[rs, sl] = (a * a).astype(a_ref.dtype)
        ff = jnp.dot(a_ref[rs, :], w2_ref[...], preferred_element_type=F32)
        o_ref[rs, :] = _layernorm(DEEPNORM_ALPHA * h + ff, g_ref[...], b_ref[...])


def _mlp(h, w1, w2, g, b):
    n = h.shape[0]
    row = pl.BlockSpec((MLP_ROW_TILE, D_MODEL), lambda i: (i, 0))
    return pl.pallas_call(
        _mlp_kernel, grid=(n // MLP_ROW_TILE,),
        in_specs=[row,
                  pl.BlockSpec((D_MODEL, D_FF), lambda i: (0, 0), pipeline_mode=pl.Buffered(1)),
                  pl.BlockSpec((D_FF, D_MODEL), lambda i: (0, 0), pipeline_mode=pl.Buffered(1)),
                  _const_spec((1, D_MODEL)), _const_spec((1, D_MODEL))],
        out_specs=row, out_shape=jax.ShapeDtypeStruct((n, D_MODEL), F32),
        scratch_shapes=[pltpu.VMEM((MLP_ROW_TILE, D_FF), BF16)],
        compiler_params=_params("parallel"), name="mlp",
    )(h, w1, w2, g, b)


def _permute_w_in(w):
    dt_lo = 3 * NA_DIM + SSD_DIM + SSD_CONV_DIM
    dt_hi = dt_lo + 2 * SSD_HEADS
    pad = jnp.zeros((w.shape[0], DT_PAD - 2 * SSD_HEADS), w.dtype)
    return jnp.concatenate([w[:, :dt_lo], w[:, dt_hi:], w[:, dt_lo:dt_hi], pad], axis=1).astype(BF16)


def kernel(x, mem, ln_in_g, ln_in_b, w_in, na_rpb, ssd_conv_w, ssd_conv_b, ssd_dt_bias, ssd_a_log, ssd_d,
           ssd_norm_w, s5_lam_re, s5_lam_im, s5_log_dt, s5_b_re, s5_b_im, s5_c_re, s5_c_im, s5_d, s5_glu_w,
           s5_glu_b, w_mix_out, ln_mix_g, ln_mix_b, xa_wq, xa_wk, xa_wv, xa_wo, ln_xa_g, ln_xa_b, mlp_w1,
           mlp_w2, ln_mlp_g, ln_mlp_b):
    bsz, t, _ = x.shape
    n = bsz * t
    rows = t // GRID_W
    assert all(t % tile == 0 for tile in (IN_ROW_TILE, MIX_ROW_TILE, XA_ROW_TILE, MLP_ROW_TILE))
    assert t % (SSD_BLOCK_CHUNKS * SSD_CHUNK) == 0 and rows % NA_Q_ROWS == 0
    assert rows >= NA_K_ROWS and t % S5_CHUNK == 0
    row1 = lambda v: v.reshape(1, -1).astype(F32)
    mem2 = mem.reshape(-1, D_MODEL)
    h = x.reshape(n, D_MODEL)
    for l in range(DEPTH):
        dt_bias = jnp.pad(row1(ssd_dt_bias[l]), ((0, 0), (0, DT_PAD - 2 * SSD_HEADS)))
        outs = _inproj(h, row1(ln_in_g), row1(ln_in_b), _permute_w_in(w_in[l]), ssd_conv_w[l].astype(F32),
                       row1(ssd_conv_b[l]), dt_bias, t, apply_ln=(l == 0))
        qkv, z, xc, bt, u_cm, dtt = outs[:6]
        if l == 0:
            h = outs[6]
        o_na = _na(qkv, _na_bias_tables(na_rpb[l].astype(F32), rows), bsz, t)

        xc3 = xc.reshape(bsz, t, SSD_CONV_DIM)
        a_log16 = row1(ssd_a_log[l])
        prev_f, prev_b = _ssd_states(xc3, bt, dtt, a_log16)
        o_ssd = _ssd_out(xc3, z.reshape(bsz, t, SSD_DIM), dtt, prev_f, prev_b, a_log16,
                         row1(jnp.repeat(ssd_d[l], HEAD_DIM)), row1(ssd_norm_w[l]))

        tables = _s5_tables(s5_lam_re[l].astype(F32), s5_lam_im[l].astype(F32), s5_log_dt[l].astype(F32),
                            s5_b_re[l].astype(F32), s5_b_im[l].astype(F32), s5_c_re[l].astype(F32),
                            s5_c_im[l].astype(F32))
        y5 = _s5(u_cm, tables, row1(s5_d[l]), bsz, t)

        h = _mixout(h, o_na, o_ssd, y5, s5_glu_w[l].astype(BF16), row1(s5_glu_b[l]),
                    w_mix_out[l].astype(BF16), row1(ln_mix_g[l]), row1(ln_mix_b[l]))
        k2, v2 = _kvproj(mem2, xa_wk[l].astype(BF16), xa_wv[l].astype(BF16))
        h = _xattn(h, k2.reshape(bsz, -1, D_MODEL), v2.reshape(bsz, -1, D_MODEL), xa_wq[l].astype(BF16),
                   xa_wo[l].astype(BF16), row1(ln_xa_g[l]), row1(ln_xa_b[l]), bsz, t)
        h = _mlp(h, mlp_w1[l].astype(BF16), mlp_w2[l].astype(BF16), row1(ln_mlp_g[l]), row1(ln_mlp_b[l]))
    return h.reshape(bsz, t, D_MODEL)
```
